```python
import math
import jax, jax.numpy as jnp
from jax import lax
import numpy as np

D_MODEL = 1024
BATCH = 4
SEQ = 4096
DEPTH = 2
DEC_BATCH = 128
DEC_SEQ = 1
PAST_LEN = 2048
PAGE_SIZE = 128

N_MIXERS = 2
N_ATTN_LAYERS = (DEPTH + 1) // 2
N_CONV_LAYERS = DEPTH // 2
ATT_HEADS = 8
ATT_HD = 64
ATT_KD = 2 * ATT_HD
ATT_VD = 2 * ATT_HD
Q_BLOCK = 128
CONV_W = 3
PEER_HEADS = 8
PEER_DK = 256
N_KEYS = 128
N_EXPERTS = N_KEYS * N_KEYS
PEER_TOPK = 16
PEER_BLOCK = 128
PLE_DIM = 256
DN_ALPHA = (2 * DEPTH) ** 0.25
DN_BETA = (8 * DEPTH) ** -0.25
LN_EPS = 1e-5
SUBLN_EPS = 1e-5
NEG_INF = -1e30

kernel_name = "diffattn_shortconv_peer_hybrid_step"


def layer_norm(x, g, b):
    xf = x.astype(jnp.float32)
    mu = jnp.mean(xf, axis=-1, keepdims=True)
    var = jnp.mean(jnp.square(xf - mu), axis=-1, keepdims=True)
    return ((xf - mu) * lax.rsqrt(var + LN_EPS) * g + b).astype(x.dtype)


def alibi_slopes():
    return jnp.exp2(-8.0 * jnp.arange(1, ATT_HEADS + 1, dtype=jnp.float32) / ATT_HEADS)


def diff_logits(q, k, q_pos, k_pos):
    s = jnp.einsum('bqhcd,bkhcd->bchqk', q, k, preferred_element_type=jnp.float32)
    s = s * (ATT_HD ** -0.5)
    dist = (q_pos[:, None] - k_pos[None, :]).astype(jnp.float32)
    s = s - alibi_slopes()[:, None, None] * dist
    return jnp.where(dist >= 0, s, NEG_INF)


def diff_weights(logits, lam):
    p = jax.nn.softmax(logits, axis=-1)
    return p[:, 0] - lam * p[:, 1]


def diff_lambda(w_lam, lam_init):
    wl = w_lam.astype(jnp.float32)
    return jnp.exp(jnp.sum(wl[0] * wl[1])) - jnp.exp(jnp.sum(wl[2] * wl[3])) + lam_init


def attn_project(x, w_qkv):
    b, s, _ = x.shape
    qkv = x @ w_qkv
    w = ATT_HEADS * ATT_KD
    q = qkv[..., :w].reshape(b, s, ATT_HEADS, 2, ATT_HD)
    k = qkv[..., w:2 * w].reshape(b, s, ATT_HEADS, ATT_KD)
    v = qkv[..., 2 * w:].reshape(b, s, ATT_HEADS, ATT_VD)
    return q, k, v


def attn_finish(o, subln_g, w_o, lam_init, dtype):
    o = o * lax.rsqrt(jnp.mean(jnp.square(o), axis=-1, keepdims=True) + SUBLN_EPS)
    o = o * subln_g * (1.0 - lam_init)
    b, q = o.shape[:2]
    return o.reshape(b, q, -1).astype(dtype) @ w_o


def diff_attn_prompt(x, w_qkv, w_lam, subln_g, w_o, lam_init):
    b, s, _ = x.shape
    q, k, v = attn_project(x, w_qkv)
    k5 = k.reshape(b, s, ATT_HEADS, 2, ATT_HD)
    lam = diff_lambda(w_lam, lam_init)
    k_pos = jnp.arange(s)

    def block(i):
        start = i * Q_BLOCK
        qb = lax.dynamic_slice_in_dim(q, start, Q_BLOCK, axis=1)
        q_pos = start + jnp.arange(Q_BLOCK)
        w = diff_weights(diff_logits(qb, k5, q_pos, k_pos), lam).astype(v.dtype)
        return jnp.einsum('bhqk,bkhv->bqhv', w, v, preferred_element_type=jnp.float32)

    o = lax.map(block, jnp.arange(s // Q_BLOCK))
    o = jnp.moveaxis(o, 0, 1).reshape(b, s, ATT_HEADS, ATT_VD)
    return attn_finish(o, subln_g, w_o, lam_init, x.dtype), k, v


def diff_attn_sample(x, cache_k, cache_v, layer, page_table, w_qkv, w_lam, subln_g, w_o, lam_init):
    b, s, _ = x.shape
    q, k, v = attn_project(x, w_qkv)
    lam = diff_lambda(w_lam, lam_init)
    past = page_table.shape[1] * cache_k.shape[2]
    k_past = cache_k[layer, page_table].reshape(b, past, ATT_HEADS, 2, ATT_HD)
    v_past = cache_v[layer, page_table].reshape(b, past, ATT_HEADS, ATT_VD)
    q_pos = past + jnp.arange(s)
    logits = jnp.concatenate([
        diff_logits(q, k_past, q_pos, jnp.arange(past)),
        diff_logits(q, k.reshape(b, s, ATT_HEADS, 2, ATT_HD), q_pos, q_pos)], axis=-1)
    w = diff_weights(logits, lam).astype(x.dtype)
    o = (jnp.einsum('bhqk,bkhv->bqhv', w[..., :past], v_past, preferred_element_type=jnp.float32)
         + jnp.einsum('bhqk,bkhv->bqhv', w[..., past:], v, preferred_element_type=jnp.float32))
    return attn_finish(o, subln_g, w_o, lam_init, x.dtype), k, v


def short_conv(x, left, w_in, conv_w, w_out):
    s = x.shape[1]
    b_g, c_g, h = jnp.split(x @ w_in, 3, axis=-1)
    u = c_g * h
    u_ext = jnp.concatenate([left.astype(u.dtype), u], axis=1)
    z = conv_w[0] * u_ext[:, 0:s]
    for j in range(1, CONV_W):
        z = z + conv_w[j] * u_ext[:, j:j + s]
    return (b_g * z) @ w_out, u_ext[:, -(CONV_W - 1):]


def peer_ffn(x, w_q, keys, u_tab, v_tab):
    shp = x.shape
    x2 = x.reshape(-1, shp[-1])
    t = x2.shape[0]
    pad = (-t) % PEER_BLOCK
    x2 = jnp.pad(x2, ((0, pad), (0, 0)))
    xb = x2.reshape(-1, PEER_BLOCK, shp[-1])

    def block(xt):
        q = (xt @ w_q).reshape(PEER_BLOCK, PEER_HEADS, 2, PEER_DK // 2)
        s = jnp.einsum('thcd,hcnd->thcn', q, keys, preferred_element_type=jnp.float32)
        sv, si = lax.top_k(s, PEER_TOPK)
        cand_s = (sv[:, :, 0, :, None] + sv[:, :, 1, None, :]).reshape(PEER_BLOCK, PEER_HEADS, PEER_TOPK ** 2)
        cand_i = (si[:, :, 0, :, None] * N_KEYS + si[:, :, 1, None, :]).reshape(PEER_BLOCK, PEER_HEADS, PEER_TOPK ** 2)
        top_s, top_p = lax.top_k(cand_s, PEER_TOPK)
        idx = jnp.take_along_axis(cand_i, top_p, axis=-1)
        g = jax.nn.softmax(top_s, axis=-1)
        a = jnp.einsum('td,thkd->thk', xt, u_tab[idx], preferred_element_type=jnp.float32)
        hk = (g * jax.nn.gelu(a, approximate=False)).astype(xt.dtype)
        return jnp.einsum('thk,thkd->td', hk, v_tab[idx])

    out = lax.map(block, xb).reshape(-1, shp[-1])[:t]
    return out.reshape(shp)


def ple_add(h, p, w_gate, w_proj):
    return h + jax.nn.sigmoid(h @ w_gate) * (p @ w_proj)


def setup_inputs(seed: int = 0) -> dict:
    key = jax.random.key(seed)
    ks = jax.random.split(key, 26)
    f32 = jnp.float32
    n_pages = PAST_LEN // PAGE_SIZE
    used = DEC_BATCH * n_pages
    pool = used + max(1, used // 4)

    def nrm(k, shape, scale):
        return scale * jax.random.normal(k, shape, f32)

    hw = ATT_HEADS * ATT_KD
    qkv_scale = jnp.concatenate([jnp.ones((2 * hw,), f32), jnp.full((ATT_HEADS * ATT_VD,), DN_BETA, f32)])
    return {
        "x_prompt": nrm(ks[0], (BATCH, SEQ, D_MODEL), 1.0),
        "x_sample": nrm(ks[1], (DEC_BATCH, DEC_SEQ, D_MODEL), 1.0),
        "cache_k": nrm(ks[2], (N_ATTN_LAYERS, pool, PAGE_SIZE, ATT_HEADS, ATT_KD), 1.0),
        "cache_v": nrm(ks[3], (N_ATTN_LAYERS, pool, PAGE_SIZE, ATT_HEADS, ATT_VD), DN_BETA),
        "state_conv": nrm(ks[4], (N_CONV_LAYERS, DEC_BATCH, CONV_W - 1, D_MODEL), 1.0),
        "page_table": jax.random.permutation(ks[5], pool)[:used].reshape(DEC_BATCH, n_pages).astype(jnp.int32),
        "p_prompt": nrm(ks[6], (DEPTH, BATCH, SEQ, PLE_DIM), 1.0),
        "p_sample": nrm(ks[7], (DEPTH, DEC_BATCH, DEC_SEQ, PLE_DIM), 1.0),
        "ln_g": 1.0 + nrm(ks[8], (DEPTH, 2, D_MODEL), 0.02),
        "ln_b": nrm(ks[9], (DEPTH, 2, D_MODEL), 0.02),
        "w_attn_qkv": nrm(ks[10], (N_ATTN_LAYERS, D_MODEL, 2 * hw + ATT_HEADS * ATT_VD), D_MODEL ** -0.5) * qkv_scale,
        "w_attn_lambda": nrm(ks[11], (N_ATTN_LAYERS, 4, ATT_HD), 0.1),
        "attn_subln_g": 1.0 + nrm(ks[12], (N_ATTN_LAYERS, ATT_VD), 0.02),
        "w_attn_o": nrm(ks[13], (N_ATTN_LAYERS, ATT_HEADS * ATT_VD, D_MODEL), DN_BETA * (ATT_HEADS * ATT_VD) ** -0.5),
        "w_conv_in": nrm(ks[14], (N_CONV_LAYERS, D_MODEL, 3 * D_MODEL), D_MODEL ** -0.5),
        "conv_w": nrm(ks[15], (N_CONV_LAYERS, CONV_W, D_MODEL), CONV_W ** -0.5),
        "w_conv_out": nrm(ks[16], (N_CONV_LAYERS, D_MODEL, D_MODEL), DN_BETA * D_MODEL ** -0.5),
        "w_peer_q": nrm(ks[17], (DEPTH, D_MODEL, PEER_HEADS * PEER_DK), D_MODEL ** -0.5),
        "peer_keys": nrm(ks[18], (DEPTH, PEER_HEADS, 2, N_KEYS, PEER_DK // 2), (PEER_DK // 2) ** -0.5),
        "peer_u": nrm(ks[19], (DEPTH, N_EXPERTS, D_MODEL), D_MODEL ** -0.5),
        "peer_v": nrm(ks[20], (DEPTH, N_EXPERTS, D_MODEL), DN_BETA),
        "w_ple_gate": nrm(ks[21], (DEPTH, D_MODEL, D_MODEL), D_MODEL ** -0.5),
        "w_ple_proj": nrm(ks[22], (DEPTH, PLE_DIM, D_MODEL), PLE_DIM ** -0.5),
    }


def reference(x_prompt, x_sample, cache_k, cache_v, state_conv, page_table, p_prompt, p_sample,
              ln_g, ln_b, w_attn_qkv, w_attn_lambda, attn_subln_g, w_attn_o,
              w_conv_in, conv_w, w_conv_out, w_peer_q, peer_keys, peer_u, peer_v,
              w_ple_gate, w_ple_proj):
    hp, hs = x_prompt, x_sample
    kp_l, vp_l, ks_l, vs_l, cp_l, cs_l = [], [], [], [], [], []
    for i in range(DEPTH):
        j = i // N_MIXERS
        if i % N_MIXERS == 0:
            lam_init = 0.8 - 0.6 * math.exp(-0.3 * i)
            yp, kp, vp = diff_attn_prompt(hp, w_attn_qkv[j], w_attn_lambda[j], attn_subln_g[j], w_attn_o[j], lam_init)
            ys, kn, vn = diff_attn_sample(hs, cache_k, cache_v, j, page_table, w_attn_qkv[j], w_attn_lambda[j],
                                          attn_subln_g[j], w_attn_o[j], lam_init)
            kp_l.append(kp); vp_l.append(vp); ks_l.append(kn); vs_l.append(vn)
        else:
            zeros_left = jnp.zeros((hp.shape[0], CONV_W - 1, hp.shape[2]), hp.dtype)
            yp, cp = short_conv(hp, zeros_left, w_conv_in[j], conv_w[j], w_conv_out[j])
            ys, cs = short_conv(hs, state_conv[j], w_conv_in[j], conv_w[j], w_conv_out[j])
            cp_l.append(cp); cs_l.append(cs)
        hp = layer_norm(DN_ALPHA * hp + yp, ln_g[i, 0], ln_b[i, 0])
        hs = layer_norm(DN_ALPHA * hs + ys, ln_g[i, 0], ln_b[i, 0])
        hp = layer_norm(DN_ALPHA * hp + peer_ffn(hp, w_peer_q[i], peer_keys[i], peer_u[i], peer_v[i]), ln_g[i, 1], ln_b[i, 1])
        hs = layer_norm(DN_ALPHA * hs + peer_ffn(hs, w_peer_q[i], peer_keys[i], peer_u[i], peer_v[i]), ln_g[i, 1], ln_b[i, 1])
        hp = ple_add(hp, p_prompt[i], w_ple_gate[i], w_ple_proj[i])
        hs = ple_add(hs, p_sample[i], w_ple_gate[i], w_ple_proj[i])
    return (hp, hs, jnp.stack(kp_l), jnp.stack(vp_l), jnp.stack(ks_l), jnp.stack(vs_l),
            jnp.stack(cp_l), jnp.stack(cs_l))
```

```python
import functools
import math

import jax
import jax.numpy as jnp
from jax import lax
from jax.experimental import pallas as pl
from jax.experimental.pallas import tpu as pltpu

BF16 = jnp.bfloat16
F32 = jnp.float32

ATT_HEADS = 8
ATT_HD = 64
ATT_W = 2 * ATT_HD
N_MIXERS = 2
CONV_W = 3
PEER_HEADS = 8
N_KEYS = 128
PEER_TOPK = 16
LN_EPS = 1e-5
SUBLN_EPS = 1e-5
NEG_INF = -1e30
LANES = 128
VMEM_LIMIT = 56 * 1024 * 1024


def _params(*sem):
    return pltpu.CompilerParams(dimension_semantics=sem, vmem_limit_bytes=VMEM_LIMIT)


def _nt_dot(a, b):
    return lax.dot_general(a, b, (((1,), (1,)), ((), ())), preferred_element_type=F32)


def _layer_norm(z, g, b):
    mu = jnp.mean(z, axis=-1, keepdims=True)
    zc = z - mu
    var = jnp.mean(zc * zc, axis=-1, keepdims=True)
    return zc * lax.rsqrt(var + LN_EPS) * g + b


def _diff_lambda(wl, lam_init):
    a = jnp.sum(wl[0:1] * wl[1:2], axis=1, keepdims=True)
    b = jnp.sum(wl[2:3] * wl[3:4], axis=1, keepdims=True)
    return jnp.exp(a) - jnp.exp(b) + lam_init


def _mm_kernel(x_ref, w_ref, o_ref):
    o_ref[...] = jnp.dot(x_ref[...].astype(BF16), w_ref[...], preferred_element_type=F32)


def matmul_slabs(x, w, *, tm, tn):
    m, k = x.shape
    n = w.shape[1]
    return pl.pallas_call(
        _mm_kernel,
        grid=(m // tm, n // tn),
        in_specs=[pl.BlockSpec((tm, k), lambda i, j: (i, 0)),
                  pl.BlockSpec((k, tn), lambda i, j: (0, j))],
        out_specs=pl.BlockSpec((None, tm, tn), lambda i, j: (j, i, 0)),
        out_shape=jax.ShapeDtypeStruct((n // tn, m, tn), F32),
        compiler_params=_params("parallel", "parallel"),
        name="matmul_slabs",
    )(x, w)


def _flash_kernel(slope_ref, wl_ref, g_ref, q_ref, k_ref, v_ref, o_ref,
                  q1_s, q2_s, m1_s, l1_s, a1_s, m2_s, l2_s, a2_s, *, tq, tk, lam_init):
    qi = pl.program_id(2)
    ki = pl.program_id(3)
    nk = pl.num_programs(3)
    q0 = qi * tq
    k0 = ki * tk

    @pl.when(ki == 0)
    def _init():
        q = q_ref[...] * (ATT_HD ** -0.5)
        lane = lax.broadcasted_iota(jnp.int32, q.shape, 1)
        q1_s[...] = jnp.where(lane < ATT_HD, q, 0.0).astype(BF16)
        q2_s[...] = jnp.where(lane >= ATT_HD, q, 0.0).astype(BF16)
        for m_s, l_s, a_s in ((m1_s, l1_s, a1_s), (m2_s, l2_s, a2_s)):
            m_s[...] = jnp.full(m_s.shape, NEG_INF, F32)
            l_s[...] = jnp.zeros(l_s.shape, F32)
            a_s[...] = jnp.zeros(a_s.shape, F32)

    @pl.when(k0 <= q0 + tq - 1)
    def _step():
        kb = k_ref[...].astype(BF16)
        vb = v_ref[...].astype(BF16)
        col = lax.broadcasted_iota(jnp.int32, (1, tk), 1)
        bias = slope_ref[...] * (k0 + col - q0).astype(F32)
        rows = lax.broadcasted_iota(jnp.int32, (tq, tk), 0) + q0
        cols = lax.broadcasted_iota(jnp.int32, (tq, tk), 1) + k0
        future = cols > rows
        for q_s, m_s, l_s, a_s in ((q1_s, m1_s, l1_s, a1_s), (q2_s, m2_s, l2_s, a2_s)):
            s = _nt_dot(q_s[...], kb) + bias
            s = jnp.where(future, NEG_INF, s)
            m_old = m_s[...]
            m_new = jnp.maximum(m_old, jnp.max(s, axis=1, keepdims=True))
            alpha = jnp.exp(m_old - m_new)
            p = jnp.exp(s - m_new)
            l_s[...] = alpha * l_s[...] + jnp.sum(p, axis=1, keepdims=True)
            a_s[...] = alpha * a_s[...] + jnp.dot(p.astype(BF16), vb, preferred_element_type=F32)
            m_s[...] = m_new

    @pl.when(ki == nk - 1)
    def _finish():
        lam = _diff_lambda(wl_ref[...], lam_init)
        o = a1_s[...] / l1_s[...] - lam * (a2_s[...] / l2_s[...])
        ms = jnp.mean(o * o, axis=1, keepdims=True)
        o_ref[...] = o * lax.rsqrt(ms + SUBLN_EPS) * g_ref[...] * (1.0 - lam_init)


def flash_prompt(qkv, slopes, w_lam, subln_g, *, batch, seq, tq, tk, lam_init):
    nq = seq // tq
    nkb = seq // tk

    def kv_row(b, qi, ki):
        return b * nkb + jnp.minimum(ki, (qi * tq + tq - 1) // tk)

    kern = functools.partial(_flash_kernel, tq=tq, tk=tk, lam_init=lam_init)
    return pl.pallas_call(
        kern,
        grid=(batch, ATT_HEADS, nq, nkb),
        in_specs=[
            pl.BlockSpec((None, 1, tk), lambda b, h, qi, ki: (h, 0, 0)),
            pl.BlockSpec((4, ATT_HD), lambda b, h, qi, ki: (0, 0)),
            pl.BlockSpec((1, ATT_W), lambda b, h, qi, ki: (0, 0)),
            pl.BlockSpec((None, tq, ATT_W), lambda b, h, qi, ki: (0, b * nq + qi, h)),
            pl.BlockSpec((None, tk, ATT_W), lambda b, h, qi, ki: (1, kv_row(b, qi, ki), h)),
            pl.BlockSpec((None, tk, ATT_W), lambda b, h, qi, ki: (2, kv_row(b, qi, ki), h)),
        ],
        out_specs=pl.BlockSpec((tq, ATT_W), lambda b, h, qi, ki: (b * nq + qi, h)),
        out_shape=jax.ShapeDtypeStruct((batch * seq, ATT_HEADS * ATT_W), F32),
        scratch_shapes=[
            pltpu.VMEM((tq, ATT_W), BF16), pltpu.VMEM((tq, ATT_W), BF16),
            pltpu.VMEM((tq, 1), F32), pltpu.VMEM((tq, 1), F32), pltpu.VMEM((tq, ATT_W), F32),
            pltpu.VMEM((tq, 1), F32), pltpu.VMEM((tq, 1), F32), pltpu.VMEM((tq, ATT_W), F32),
        ],
        compiler_params=_params("parallel", "parallel", "parallel", "arbitrary"),
        name="flash_prompt",
    )(slopes, w_lam, subln_g, qkv, qkv, qkv)


def _decode_kernel(pt_ref, q_ref, kn_ref, vn_ref, k_ref, v_ref, hm_ref, m8_ref, sl_ref,
                   g_ref, wl_ref, o_ref, qm_s, m_s, l_s, a_s, *, page, past, lam_init):
    del pt_ref
    p = pl.program_id(1)
    n_pages = pl.num_programs(1)
    nh = ATT_HEADS

    @pl.when(p == 0)
    def _init():
        qm = q_ref[...] * (ATT_HD ** -0.5) * hm_ref[...]
        qm_s[...] = qm.astype(BF16)
        m_s[...] = jnp.full(m_s.shape, NEG_INF, F32)
        l_s[...] = jnp.zeros(l_s.shape, F32)
        a_s[...] = jnp.zeros(a_s.shape, F32)

    kb = k_ref[...].astype(BF16)
    vb = v_ref[...].astype(BF16)
    s = _nt_dot(qm_s[...], kb)
    kpos = p * page + lax.broadcasted_iota(jnp.int32, (1, page), 1)
    s = s - sl_ref[...] * (past - kpos).astype(F32)
    m_old = m_s[...]
    m_new = jnp.maximum(m_old, jnp.max(s, axis=1, keepdims=True))
    alpha = jnp.exp(m_old - m_new)
    pe = jnp.exp(s - m_new)
    l_s[...] = alpha * l_s[...] + jnp.sum(pe, axis=1, keepdims=True)
    a_s[...] = alpha * a_s[...] + jnp.dot(pe.astype(BF16), vb, preferred_element_type=F32)
    m_s[...] = m_new

    @pl.when(p == n_pages - 1)
    def _finish():
        qm = q_ref[...] * (ATT_HD ** -0.5) * hm_ref[...]
        s_self = jnp.sum(qm * kn_ref[...], axis=1, keepdims=True)
        m_o = m_s[...]
        m_n = jnp.maximum(m_o, s_self)
        al = jnp.exp(m_o - m_n)
        p_self = jnp.exp(s_self - m_n)
        l = al * l_s[...] + p_self
        acc = al * a_s[...] + p_self * vn_ref[...]
        lam = _diff_lambda(wl_ref[...], lam_init)
        d = acc[0:nh] / l[0:nh] - lam * (acc[nh:2 * nh] / l[nh:2 * nh])
        d = d * m8_ref[...]
        ms = jnp.sum(d * d, axis=1, keepdims=True) * (1.0 / ATT_W)
        d = d * lax.rsqrt(ms + SUBLN_EPS)
        o_ref[...] = jnp.sum(d, axis=0, keepdims=True) * g_ref[...] * (1.0 - lam_init)


def decode_sample(page_table, q, k_new, v_new, cache_k, cache_v, w_lam, subln_g, *, layer, lam_init):
    bs, n_pages = page_table.shape
    n_layers, pool, page = cache_k.shape[:3]
    width = ATT_HEADS * ATT_W
    past = n_pages * page
    ck = cache_k.reshape(n_layers * pool, page, width)
    cv = cache_v.reshape(n_layers * pool, page, width)
    col = jnp.arange(width)
    row = jnp.arange(2 * ATT_HEADS)
    hm = ((col[None, :] // ATT_W == row[:, None] % ATT_HEADS)
          & ((col[None, :] % ATT_W) // ATT_HD == row[:, None] // ATT_HEADS)).astype(F32)
    m8 = (col[None, :] // ATT_W == jnp.arange(ATT_HEADS)[:, None]).astype(F32)
    slopes = jnp.exp2(-8.0 * jnp.arange(1, ATT_HEADS + 1, dtype=F32) / ATT_HEADS)
    sl = jnp.broadcast_to(jnp.tile(slopes, 2)[:, None], (2 * ATT_HEADS, page))
    g_t = jnp.tile(subln_g, ATT_HEADS)[None, :]
    base = layer * pool

    row_spec = pl.BlockSpec((None, 1, width), lambda b, p, pt: (b, 0, 0))
    const = lambda shape: pl.BlockSpec(shape, lambda b, p, pt: (0,) * len(shape))
    page_spec = pl.BlockSpec((None, page, width), lambda b, p, pt: (base + pt[b, p], 0, 0))
    kern = functools.partial(_decode_kernel, page=page, past=past, lam_init=lam_init)
    out = pl.pallas_call(
        kern,
        grid_spec=pltpu.PrefetchScalarGridSpec(
            num_scalar_prefetch=1,
            grid=(bs, n_pages),
            in_specs=[row_spec, row_spec, row_spec, page_spec, page_spec,
                      const((2 * ATT_HEADS, width)), const((ATT_HEADS, width)),
                      const((2 * ATT_HEADS, page)), const((1, width)), const((4, ATT_HD))],
            out_specs=pl.BlockSpec((None, 1, width), lambda b, p, pt: (b, 0, 0)),
            scratch_shapes=[pltpu.VMEM((2 * ATT_HEADS, width), BF16),
                            pltpu.VMEM((2 * ATT_HEADS, 1), F32),
                            pltpu.VMEM((2 * ATT_HEADS, 1), F32),
                            pltpu.VMEM((2 * ATT_HEADS, width), F32)],
        ),
        out_shape=jax.ShapeDtypeStruct((bs, 1, width), F32),
        compiler_params=_params("parallel", "arbitrary"),
        name="decode_sample",
    )(page_table, q[:, None, :], k_new[:, None, :], v_new[:, None, :], ck, cv, hm, m8, sl, g_t, w_lam)
    return out[:, 0, :]


def _mm_res_ln_kernel(x_ref, w_ref, res_ref, g_ref, b_ref, o_ref, *, alpha):
    y = jnp.dot(x_ref[...].astype(BF16), w_ref[...], preferred_element_type=F32)
    o_ref[...] = _layer_norm(alpha * res_ref[...] + y, g_ref[...], b_ref[...])


def mm_res_ln(x, w, res, g, b, *, alpha, tm):
    m, k = x.shape
    n = w.shape[1]
    rows = lambda width: pl.BlockSpec((tm, width), lambda i: (i, 0))
    full = lambda shape: pl.BlockSpec(shape, lambda i: (0, 0))
    return pl.pallas_call(
        functools.partial(_mm_res_ln_kernel, alpha=alpha),
        grid=(m // tm,),
        in_specs=[rows(k), full((k, n)), rows(n), full((1, n)), full((1, n))],
        out_specs=rows(n),
        out_shape=jax.ShapeDtypeStruct((m, n), F32),
        compiler_params=_params("parallel"),
        name="mm_res_ln",
    )(x, w, res, g[None, :], b[None, :])


CARRY_ROWS = 8


def _conv_kernel(*refs, alpha, tm, chained):
    if chained:
        x_ref, win_ref, cw_ref, wout_ref, g_ref, b_ref, o_ref, u_ref, carry_s = refs
    else:
        x_ref, l0_ref, l1_ref, win_ref, cw_ref, wout_ref, g_ref, b_ref, o_ref, u_ref = refs
    d = x_ref.shape[1]
    x = x_ref[...]
    bch = jnp.dot(x.astype(BF16), win_ref[...], preferred_element_type=F32)
    b_g = bch[:, 0:d]
    u = bch[:, d:2 * d] * bch[:, 2 * d:3 * d]
    cw = cw_ref[...]
    if chained:
        i = pl.program_id(1)

        @pl.when(i == 0)
        def _zero_left():
            carry_s[...] = jnp.zeros(carry_s.shape, F32)

        prev = carry_s[...]
        row = lax.broadcasted_iota(jnp.int32, u.shape, 0)
        last = prev[CARRY_ROWS - 1:CARRY_ROWS]
        u1 = jnp.where(row == 0, last, pltpu.roll(u, 1, 0))
        u2 = jnp.where(row == 0, prev[CARRY_ROWS - 2:CARRY_ROWS - 1],
                       jnp.where(row == 1, last, pltpu.roll(u, 2, 0)))
        tail = u[tm - CARRY_ROWS:tm]
        carry_s[...] = tail
        u_ref[...] = tail
    else:
        u2 = l0_ref[...]
        u1 = l1_ref[...]
        u_ref[...] = u
    z = cw[0:1] * u2 + cw[1:2] * u1 + cw[2:3] * u
    y = jnp.dot((b_g * z).astype(BF16), wout_ref[...], preferred_element_type=F32)
    o_ref[...] = _layer_norm(alpha * x + y, g_ref[...], b_ref[...])


def conv_prompt(x, w_in, conv_w, w_out, g, b, *, batch, seq, alpha, tm):
    d = x.shape[1]
    nt = seq // tm
    rows = pl.BlockSpec((tm, d), lambda bi, i: (bi * nt + i, 0))
    full = lambda shape: pl.BlockSpec(shape, lambda bi, i: (0, 0))
    return pl.pallas_call(
        functools.partial(_conv_kernel, alpha=alpha, tm=tm, chained=True),
        grid=(batch, nt),
        in_specs=[rows, full((d, 3 * d)), full((CONV_W, d)), full((d, d)), full((1, d)), full((1, d))],
        out_specs=[rows, pl.BlockSpec((None, CARRY_ROWS, d), lambda bi, i: (bi, 0, 0))],
        out_shape=[jax.ShapeDtypeStruct((batch * seq, d), F32),
                   jax.ShapeDtypeStruct((batch, CARRY_ROWS, d), F32)],
        scratch_shapes=[pltpu.VMEM((CARRY_ROWS, d), F32)],
        compiler_params=_params("parallel", "arbitrary"),
        name="conv_prompt",
    )(x, w_in, conv_w, w_out, g[None, :], b[None, :])


def conv_sample(x, left0, left1, w_in, conv_w, w_out, g, b, *, alpha):
    m, d = x.shape
    full = lambda shape: pl.BlockSpec(shape, lambda i: (0, 0))
    return pl.pallas_call(
        functools.partial(_conv_kernel, alpha=alpha, tm=m, chained=False),
        grid=(1,),
        in_specs=[full((m, d)), full((m, d)), full((m, d)), full((d, 3 * d)), full((CONV_W, d)),
                  full((d, d)), full((1, d)), full((1, d))],
        out_specs=[full((m, d)), full((m, d))],
        out_shape=[jax.ShapeDtypeStruct((m, d), F32), jax.ShapeDtypeStruct((m, d), F32)],
        compiler_params=_params("arbitrary"),
        name="conv_sample",
    )(x, left0, left1, w_in, conv_w, w_out, g[None, :], b[None, :])


def _top_values(x, count, store):
    for r in range(count):
        mx = jnp.max(x, axis=0, keepdims=True)
        store(r, mx)
        if r + 1 < count:
            x = jnp.where(x == mx, -jnp.inf, x)


def _select_kernel(q_ref, keys_ref, thr_ref, e1_ref, s2_ref, e2_ref, s_s, sv_s):
    k = PEER_TOPK
    dh = keys_ref.shape[2]
    for hc in range(2 * PEER_HEADS):
        per_slab = q_ref.shape[2] // dh
        qb = q_ref[hc // per_slab, :, (hc % per_slab) * dh:(hc % per_slab + 1) * dh].astype(BF16)
        s = _nt_dot(keys_ref[hc], qb)
        s_s[hc] = s

        def store(r, mx, hc=hc):
            sv_s[hc, r:r + 1, :] = mx

        _top_values(s, k, store)

    for h in range(PEER_HEADS):
        sv1 = sv_s[2 * h]
        sv2 = sv_s[2 * h + 1]
        cand = jnp.concatenate([sv1[a:a + 1] + sv2 for a in range(k)], axis=0)
        tau_box = []
        _top_values(cand, k, lambda r, mx: tau_box.append(mx))
        tau = tau_box[-1]
        top = sv1[0:1] + sv2[0:1]
        z = jnp.sum(jnp.where(cand >= tau, jnp.exp(cand - top), 0.0), axis=0, keepdims=True)
        s1 = s_s[2 * h]
        s2 = s_s[2 * h + 1]
        thr = jnp.full(s1.shape, jnp.inf, F32)
        for b in range(k):
            sb = sv2[b:b + 1]
            thr = jnp.minimum(thr, jnp.where(s1 + sb >= tau, sb, jnp.inf))
        thr_ref[0, h] = thr
        e1_ref[0, h] = jnp.exp(s1 - sv1[0:1]) / z
        s2_ref[0, h] = s2
        e2_ref[0, h] = jnp.exp(s2 - sv2[0:1])


def peer_select(q_slabs, keys):
    n_slab, t, slab_w = q_slabs.shape
    nhc, _, dh = keys.shape
    nchunk = t // LANES
    out_sds = jax.ShapeDtypeStruct((nchunk, PEER_HEADS, N_KEYS, LANES), F32)
    out_spec = pl.BlockSpec((1, PEER_HEADS, N_KEYS, LANES), lambda i: (i, 0, 0, 0))
    return pl.pallas_call(
        _select_kernel,
        grid=(nchunk,),
        in_specs=[pl.BlockSpec((n_slab, LANES, slab_w), lambda i: (0, i, 0)),
                  pl.BlockSpec((nhc, N_KEYS, dh), lambda i: (0, 0, 0))],
        out_specs=[out_spec] * 4,
        out_shape=[out_sds] * 4,
        scratch_shapes=[pltpu.VMEM((nhc, N_KEYS, LANES), F32),
                        pltpu.VMEM((nhc, PEER_TOPK, LANES), F32)],
        compiler_params=_params("parallel"),
        name="peer_select",
    )(q_slabs, keys)


def _gelu(a):
    return 0.5 * a * (1.0 + lax.erf(a * (2.0 ** -0.5)))


def _peer_dense_kernel(xt_ref, u_ref, vt_ref, thr_ref, e1_ref, s2_ref, e2_ref, res_ref,
                       g_ref, b_ref, o_ref, a_s, w_s, acc_s, *, alpha, te, tt):
    e = pl.program_id(1)
    ne = pl.num_programs(1)
    n_i = te // N_KEYS

    @pl.when(e == 0)
    def _init():
        acc_s[...] = jnp.zeros(acc_s.shape, F32)

    a_s[...] = jnp.dot(u_ref[...], xt_ref[...], preferred_element_type=F32)

    def body(ii, carry):
        i = e * n_i + ii
        r0 = pl.multiple_of(ii * N_KEYS, N_KEYS)
        for tc in range(tt // LANES):
            lanes = slice(tc * LANES, (tc + 1) * LANES)
            gate = jnp.zeros((N_KEYS, LANES), F32)
            for h in range(PEER_HEADS):
                thr = thr_ref[tc, h, pl.ds(i, 1), :]
                e1 = e1_ref[tc, h, pl.ds(i, 1), :]
                gate = gate + jnp.where(s2_ref[tc, h] >= thr, e2_ref[tc, h], 0.0) * e1
            a = a_s[pl.ds(r0, N_KEYS), lanes]
            w_s[pl.ds(r0, N_KEYS), lanes] = (gate * _gelu(a)).astype(BF16)
        return carry

    lax.fori_loop(0, n_i, body, 0)
    acc_s[...] += jnp.dot(vt_ref[...], w_s[...], preferred_element_type=F32)

    @pl.when(e == ne - 1)
    def _finish():
        y = acc_s[...].T
        o_ref[...] = _layer_norm(alpha * res_ref[...] + y, g_ref[...], b_ref[...])


def peer_dense(xt, u, vt, sel, res, g, b, *, alpha, tt, te):
    d, t = xt.shape
    n_exp = u.shape[0]
    nchunk = tt // LANES
    sel_spec = pl.BlockSpec((nchunk, PEER_HEADS, N_KEYS, LANES), lambda ti, e: (ti, 0, 0, 0))
    full = lambda shape: pl.BlockSpec(shape, lambda ti, e: (0, 0))
    return pl.pallas_call(
        functools.partial(_peer_dense_kernel, alpha=alpha, te=te, tt=tt),
        grid=(t // tt, n_exp // te),
        in_specs=[pl.BlockSpec((d, tt), lambda ti, e: (0, ti)),
                  pl.BlockSpec((te, d), lambda ti, e: (e, 0)),
                  pl.BlockSpec((d, te), lambda ti, e: (0, e)),
                  sel_spec, sel_spec, sel_spec, sel_spec,
                  pl.BlockSpec((tt, d), lambda ti, e: (ti, 0)),
                  full((1, d)), full((1, d))],
        out_specs=pl.BlockSpec((tt, d), lambda ti, e: (ti, 0)),
        out_shape=jax.ShapeDtypeStruct((t, d), F32),
        scratch_shapes=[pltpu.VMEM((te, tt), F32), pltpu.VMEM((te, tt), BF16),
                        pltpu.VMEM((d, tt), F32)],
        compiler_params=_params("parallel", "arbitrary"),
        name="peer_dense",
    )(xt, u, vt, *sel, res, g[None, :], b[None, :])


def _ple_kernel(h_ref, p_ref, wg_ref, wp_ref, o_ref):
    h = h_ref[...]
    gate = jax.nn.sigmoid(jnp.dot(h.astype(BF16), wg_ref[...], preferred_element_type=F32))
    proj = jnp.dot(p_ref[...].astype(BF16), wp_ref[...], preferred_element_type=F32)
    o_ref[...] = h + gate * proj


def ple_add(h, p, wg, wp, *, tm):
    m, d = h.shape
    pd = p.shape[1]
    rows = lambda width: pl.BlockSpec((tm, width), lambda i: (i, 0))
    full = lambda shape: pl.BlockSpec(shape, lambda i: (0, 0))
    return pl.pallas_call(
        _ple_kernel,
        grid=(m // tm,),
        in_specs=[rows(d), rows(pd), full((d, d)), full((pd, d))],
        out_specs=rows(d),
        out_shape=jax.ShapeDtypeStruct((m, d), F32),
        compiler_params=_params("parallel"),
        name="ple_add",
    )(h, p, wg, wp)


def _largest_tile(n, candidates):
    for c in candidates:
        if n % c == 0:
            return c
    raise ValueError(f"no tile in {candidates} divides {n}")


def kernel(x_prompt, x_sample, cache_k, cache_v, state_conv, page_table, p_prompt, p_sample,
           ln_g, ln_b, w_attn_qkv, w_attn_lambda, attn_subln_g, w_attn_o,
           w_conv_in, conv_w, w_conv_out, w_peer_q, peer_keys, peer_u, peer_v,
           w_ple_gate, w_ple_proj):
    batch, seq, d = x_prompt.shape
    bs = x_sample.shape[0]
    assert x_sample.shape[1] == 1
    depth = ln_g.shape[0]
    tp = batch * seq
    t = tp + bs
    alpha = (2 * depth) ** 0.25
    width = ATT_HEADS * ATT_W

    tm = _largest_tile(t, (512, 384, 256, 128))
    tseq = _largest_tile(seq, (512, 256, 128))
    tt = _largest_tile(t, (512, 384, 256, 128))
    te = 512

    h = jnp.concatenate([x_prompt.reshape(tp, d), x_sample.reshape(bs, d)], axis=0)
    slopes = jnp.exp2(-8.0 * jnp.arange(1, ATT_HEADS + 1, dtype=F32) / ATT_HEADS)
    slopes_b = jnp.broadcast_to(slopes[:, None, None], (ATT_HEADS, 1, tseq))

    kp_l, vp_l, ks_l, vs_l, cp_l, cs_l = [], [], [], [], [], []
    for i in range(depth):
        j = i // N_MIXERS
        if i % N_MIXERS == 0:
            lam_init = 0.8 - 0.6 * math.exp(-0.3 * i)
            qkv = matmul_slabs(h, w_attn_qkv[j].astype(BF16), tm=tm, tn=width)
            g_sub = attn_subln_g[j]
            o_p = flash_prompt(qkv, slopes_b, w_attn_lambda[j], g_sub[None, :], batch=batch, seq=seq,
                               tq=tseq, tk=tseq, lam_init=lam_init)
            o_s = decode_sample(page_table, qkv[0, tp:], qkv[1, tp:], qkv[2, tp:], cache_k, cache_v,
                                w_attn_lambda[j], g_sub, layer=j, lam_init=lam_init)
            o = jnp.concatenate([o_p, o_s], axis=0)
            h = mm_res_ln(o, w_attn_o[j].astype(BF16), h, ln_g[i, 0], ln_b[i, 0], alpha=alpha, tm=tm)
            kp_l.append(qkv[1, :tp].reshape(batch, seq, ATT_HEADS, ATT_W))
            vp_l.append(qkv[2, :tp].reshape(batch, seq, ATT_HEADS, ATT_W))
            ks_l.append(qkv[1, tp:].reshape(bs, 1, ATT_HEADS, ATT_W))
            vs_l.append(qkv[2, tp:].reshape(bs, 1, ATT_HEADS, ATT_W))
        else:
            w_in = w_conv_in[j].astype(BF16)
            w_out = w_conv_out[j].astype(BF16)
            hp, tail = conv_prompt(h[:tp], w_in, conv_w[j], w_out, ln_g[i, 0], ln_b[i, 0],
                                   batch=batch, seq=seq, alpha=alpha, tm=tseq)
            left = state_conv[j]
            hs, u_s = conv_sample(h[tp:], left[:, 0], left[:, 1], w_in, conv_w[j], w_out,
                                  ln_g[i, 0], ln_b[i, 0], alpha=alpha)
            h = jnp.concatenate([hp, hs], axis=0)
            cp_l.append(tail[:, CARRY_ROWS - (CONV_W - 1):])
            cs_l.append(jnp.stack([left[:, 1], u_s], axis=1))

        n_hc = 2 * PEER_HEADS
        dh = peer_keys.shape[-1]
        q_slabs = matmul_slabs(h, w_peer_q[i].astype(BF16), tm=tm, tn=d)
        keys = peer_keys[i].reshape(n_hc, N_KEYS, dh).astype(BF16)
        sel = peer_select(q_slabs, keys)
        h = peer_dense(h.T.astype(BF16), peer_u[i].astype(BF16), peer_v[i].T.astype(BF16), sel, h,
                       ln_g[i, 1], ln_b[i, 1], alpha=alpha, tt=tt, te=te)

        p = jnp.concatenate([p_prompt[i].reshape(tp, -1), p_sample[i].reshape(bs, -1)], axis=0)
        h = ple_add(h, p, w_ple_gate[i].astype(BF16), w_ple_proj[i].astype(BF16), tm=tm)

    return (h[:tp].reshape(batch, seq, d), h[tp:].reshape(bs, 1, d),
            jnp.stack(kp_l), jnp.stack(vp_l), jnp.stack(ks_l), jnp.stack(vs_l),
            jnp.stack(cp_l), jnp.stack(cs_l))
```

```python
import functools
import math

import jax
import jax.numpy as jnp
from jax import lax
from jax.experimental import pallas as pl
from jax.experimental.pallas import tpu as pltpu

BF16 = jnp.bfloat16
F32 = jnp.float32

ATT_HEADS = 8
ATT_HD = 64
ATT_W = 2 * ATT_HD
N_MIXERS = 2
CONV_W = 3
PEER_HEADS = 8
N_KEYS = 128
PEER_TOPK = 16
LN_EPS = 1e-5
SUBLN_EPS = 1e-5
NEG_INF = -1e30
LANES = 128
SUBLANES = 8
VMEM_LIMIT = 56 * 1024 * 1024


def _params(*sem, flags=None):
    return pltpu.CompilerParams(dimension_semantics=sem, vmem_limit_bytes=VMEM_LIMIT, flags=flags)


def _nt_dot(a, b):
    return lax.dot_general(a, b, (((1,), (1,)), ((), ())), preferred_element_type=F32)


def _layer_norm(z, g, b):
    mu = jnp.mean(z, axis=-1, keepdims=True)
    zc = z - mu
    var = jnp.mean(zc * zc, axis=-1, keepdims=True)
    return zc * lax.rsqrt(var + LN_EPS) * g + b


def _diff_lambda(wl, lam_init):
    a = jnp.sum(wl[0:1] * wl[1:2], axis=1, keepdims=True)
    b = jnp.sum(wl[2:3] * wl[3:4], axis=1, keepdims=True)
    return jnp.exp(a) - jnp.exp(b) + lam_init


def _mm_kernel(x_ref, w_ref, o_ref):
    o_ref[...] = jnp.dot(x_ref[...].astype(BF16), w_ref[...], preferred_element_type=F32)


def matmul_slabs(x, w, *, tm, tn):
    m, k = x.shape
    n = w.shape[1]
    return pl.pallas_call(
        _mm_kernel,
        grid=(m // tm, n // tn),
        in_specs=[pl.BlockSpec((tm, k), lambda i, j: (i, 0)),
                  pl.BlockSpec((k, tn), lambda i, j: (0, j))],
        out_specs=pl.BlockSpec((None, tm, tn), lambda i, j: (j, i, 0)),
        out_shape=jax.ShapeDtypeStruct((n // tn, m, tn), F32),
        compiler_params=_params("parallel", "parallel"),
        name="matmul_slabs",
    )(x, w)


def _flash_kernel(slope_ref, wl_ref, g_ref, q_ref, k_ref, v_ref, o_ref,
                  q1_s, q2_s, m1_s, l1_s, a1_s, m2_s, l2_s, a2_s, *, tq, tk, lam_init):
    qi = pl.program_id(2)
    ki = pl.program_id(3)
    nk = pl.num_programs(3)
    q0 = qi * tq
    k0 = ki * tk

    @pl.when(ki == 0)
    def _init():
        q = q_ref[...] * (ATT_HD ** -0.5)
        lane = lax.broadcasted_iota(jnp.int32, q.shape, 1)
        q1_s[...] = jnp.where(lane < ATT_HD, q, 0.0).astype(BF16)
        q2_s[...] = jnp.where(lane >= ATT_HD, q, 0.0).astype(BF16)
        for m_s, l_s, a_s in ((m1_s, l1_s, a1_s), (m2_s, l2_s, a2_s)):
            m_s[...] = jnp.full(m_s.shape, NEG_INF, F32)
            l_s[...] = jnp.zeros(l_s.shape, F32)
            a_s[...] = jnp.zeros(a_s.shape, F32)

    def step(on_diagonal):
        kb = k_ref[...].astype(BF16)
        vb = v_ref[...].astype(BF16)
        col = lax.broadcasted_iota(jnp.int32, (1, tk), 1)
        bias = slope_ref[...] * (k0 + col - q0).astype(F32)
        if on_diagonal:
            rows = lax.broadcasted_iota(jnp.int32, (tq, tk), 0) + q0
            cols = lax.broadcasted_iota(jnp.int32, (tq, tk), 1) + k0
            future = cols > rows
        for q_s, m_s, l_s, a_s in ((q1_s, m1_s, l1_s, a1_s), (q2_s, m2_s, l2_s, a2_s)):
            s = _nt_dot(q_s[...], kb) + bias
            if on_diagonal:
                s = jnp.where(future, NEG_INF, s)
            m_old = m_s[...]
            m_new = jnp.maximum(m_old, jnp.max(s, axis=1, keepdims=True))
            alpha = jnp.exp(m_old - m_new)
            p = jnp.exp(s - jnp.tile(m_new, (1, tk // LANES)))
            l_s[...] = alpha * l_s[...] + jnp.sum(p, axis=1, keepdims=True)
            a_s[...] = alpha * a_s[...] + jnp.dot(p.astype(BF16), vb, preferred_element_type=F32)
            m_s[...] = m_new

    pl.when(ki < qi)(functools.partial(step, False))
    pl.when(ki == qi)(functools.partial(step, True))

    @pl.when(ki == nk - 1)
    def _finish():
        lam = _diff_lambda(wl_ref[...], lam_init)
        o = a1_s[...] / l1_s[...] - lam * (a2_s[...] / l2_s[...])
        ms = jnp.mean(o * o, axis=1, keepdims=True)
        o_ref[...] = o * lax.rsqrt(ms + SUBLN_EPS) * g_ref[...] * (1.0 - lam_init)


def flash_prompt(qkv, slopes, w_lam, subln_g, *, batch, seq, tq, tk, lam_init):
    assert tq == tk
    nq = seq // tq
    nkb = seq // tk

    def kv_row(b, qi, ki):
        return b * nkb + jnp.minimum(ki, qi)

    kern = functools.partial(_flash_kernel, tq=tq, tk=tk, lam_init=lam_init)
    return pl.pallas_call(
        kern,
        grid=(batch, ATT_HEADS, nq, nkb),
        in_specs=[
            pl.BlockSpec((None, 1, tk), lambda b, h, qi, ki: (h, 0, 0)),
            pl.BlockSpec((4, ATT_HD), lambda b, h, qi, ki: (0, 0)),
            pl.BlockSpec((1, ATT_W), lambda b, h, qi, ki: (0, 0)),
            pl.BlockSpec((None, tq, ATT_W), lambda b, h, qi, ki: (0, b * nq + qi, h)),
            pl.BlockSpec((None, tk, ATT_W), lambda b, h, qi, ki: (1, kv_row(b, qi, ki), h)),
            pl.BlockSpec((None, tk, ATT_W), lambda b, h, qi, ki: (2, kv_row(b, qi, ki), h)),
        ],
        out_specs=pl.BlockSpec((tq, ATT_W), lambda b, h, qi, ki: (b * nq + qi, h)),
        out_shape=jax.ShapeDtypeStruct((batch * seq, ATT_HEADS * ATT_W), F32),
        scratch_shapes=[
            pltpu.VMEM((tq, ATT_W), BF16), pltpu.VMEM((tq, ATT_W), BF16),
            pltpu.VMEM((tq, LANES), F32), pltpu.VMEM((tq, LANES), F32), pltpu.VMEM((tq, ATT_W), F32),
            pltpu.VMEM((tq, LANES), F32), pltpu.VMEM((tq, LANES), F32), pltpu.VMEM((tq, ATT_W), F32),
        ],
        compiler_params=_params("parallel", "parallel", "parallel", "arbitrary"),
        name="flash_prompt",
    )(slopes, w_lam, subln_g, qkv, qkv, qkv)


DECODE_PAGES_PER_STEP = 4


def _decode_kernel(pt_ref, q_ref, kn_ref, vn_ref, *rest, n_grp, page, past, lam_init):
    del pt_ref
    k_refs = rest[:n_grp]
    v_refs = rest[n_grp:2 * n_grp]
    bsel_ref, slope_ref, alibi_ref, g_ref, wl_ref, o_ref, m_s, l_s, a1_s, a2_s = rest[2 * n_grp:]
    pg = pl.program_id(1)
    n_steps = pl.num_programs(1)
    nh = ATT_HEADS
    q8 = q_ref[...] * (ATT_HD ** -0.5)

    def half_sums(prod):
        return jnp.dot(prod.astype(BF16), bsel_ref[...], preferred_element_type=F32)

    @pl.when(pg == 0)
    def _init():
        m_s[...] = jnp.full(m_s.shape, NEG_INF, F32)
        l_s[...] = jnp.zeros(l_s.shape, F32)
        a1_s[...] = jnp.zeros(a1_s.shape, F32)
        a2_s[...] = jnp.zeros(a2_s.shape, F32)

    slope = slope_ref[...]
    logits, shifts = [], []
    m_new = m_s[...]
    for g in range(n_grp):
        prod = (k_refs[g][...] * q8[None]).reshape(page * nh, ATT_W)
        s3 = half_sums(prod).reshape(page, nh, 2 * ATT_W) + alibi_ref[...]
        shift = slope * (past - (pg * n_grp + g) * page).astype(F32)
        m_new = jnp.maximum(m_new, jnp.max(s3, axis=0) - shift)
        logits.append(s3)
        shifts.append(shift)
    alpha = jnp.exp(m_s[...] - m_new)
    l = alpha * l_s[...]
    acc1 = alpha[:, :ATT_W] * a1_s[...]
    acc2 = alpha[:, ATT_W:] * a2_s[...]
    for g in range(n_grp):
        pe = jnp.exp(logits[g] - (m_new + shifts[g])[None])
        v3 = v_refs[g][...]
        l = l + jnp.sum(pe, axis=0)
        acc1 = acc1 + jnp.sum(pe[:, :, :ATT_W] * v3, axis=0)
        acc2 = acc2 + jnp.sum(pe[:, :, ATT_W:] * v3, axis=0)
    m_s[...] = m_new
    l_s[...] = l
    a1_s[...] = acc1
    a2_s[...] = acc2

    @pl.when(pg == n_steps - 1)
    def _finish():
        s_self = half_sums(q8 * kn_ref[...])
        m_n = jnp.maximum(m_new, s_self)
        al = jnp.exp(m_new - m_n)
        p_self = jnp.exp(s_self - m_n)
        lf = al * l + p_self
        vn = vn_ref[...]
        o1 = (al[:, :ATT_W] * acc1 + p_self[:, :ATT_W] * vn) / lf[:, :ATT_W]
        o2 = (al[:, ATT_W:] * acc2 + p_self[:, ATT_W:] * vn) / lf[:, ATT_W:]
        lam = _diff_lambda(wl_ref[...], lam_init)
        d = o1 - lam * o2
        ms = jnp.mean(d * d, axis=1, keepdims=True)
        o_ref[...] = d * lax.rsqrt(ms + SUBLN_EPS) * g_ref[...] * (1.0 - lam_init)


def decode_sample(page_table, q, k_new, v_new, cache_k, cache_v, w_lam, subln_g, *, layer, lam_init):
    bs, n_pages = page_table.shape
    page = cache_k.shape[2]
    nh = ATT_HEADS
    past = n_pages * page
    n_grp = DECODE_PAGES_PER_STEP
    assert n_pages % n_grp == 0
    half = jnp.arange(ATT_W)[:, None] // ATT_HD == jnp.arange(2 * ATT_W)[None, :] // ATT_W
    bsel = half.astype(BF16)
    slopes = jnp.exp2(-8.0 * jnp.arange(1, nh + 1, dtype=F32) / nh)
    slope = jnp.broadcast_to(slopes[:, None], (nh, 2 * ATT_W))
    alibi = jnp.arange(page, dtype=F32)[:, None, None] * slope[None]

    row_spec = pl.BlockSpec((None, nh, ATT_W), lambda b, p, pt: (b, 0, 0))
    const = lambda shape: pl.BlockSpec(shape, lambda b, p, pt: (0,) * len(shape))

    def page_spec(g):
        return pl.BlockSpec((None, None, page, nh, ATT_W),
                            lambda b, p, pt: (layer, pt[b, p * n_grp + g], 0, 0, 0))

    pages = [page_spec(g) for g in range(n_grp)]
    kern = functools.partial(_decode_kernel, n_grp=n_grp, page=page, past=past, lam_init=lam_init)
    return pl.pallas_call(
        kern,
        grid_spec=pltpu.PrefetchScalarGridSpec(
            num_scalar_prefetch=1,
            grid=(bs, n_pages // n_grp),
            in_specs=[row_spec, row_spec, row_spec] + pages + pages + [
                const((ATT_W, 2 * ATT_W)), const((nh, 2 * ATT_W)), const((page, nh, 2 * ATT_W)),
                const((1, ATT_W)), const((4, ATT_HD))],
            out_specs=pl.BlockSpec((None, nh, ATT_W), lambda b, p, pt: (b, 0, 0)),
            scratch_shapes=[pltpu.VMEM((nh, 2 * ATT_W), F32), pltpu.VMEM((nh, 2 * ATT_W), F32),
                            pltpu.VMEM((nh, ATT_W), F32), pltpu.VMEM((nh, ATT_W), F32)],
        ),
        out_shape=jax.ShapeDtypeStruct((bs, nh, ATT_W), F32),
        compiler_params=_params("parallel", "arbitrary"),
        name="decode_sample",
    )(page_table, q, k_new, v_new, *([cache_k] * n_grp), *([cache_v] * n_grp),
      bsel, slope, alibi, subln_g[None, :], w_lam)


def _mm_res_ln_kernel(x_ref, w_ref, res_ref, g_ref, b_ref, o_ref, *, alpha):
    y = jnp.dot(x_ref[...].astype(BF16), w_ref[...], preferred_element_type=F32)
    o_ref[...] = _layer_norm(alpha * res_ref[...] + y, g_ref[...], b_ref[...])


def mm_res_ln(x, w, res, g, b, *, alpha, tm):
    m, k = x.shape
    n = w.shape[1]
    rows = lambda width: pl.BlockSpec((tm, width), lambda i: (i, 0))
    full = lambda shape: pl.BlockSpec(shape, lambda i: (0, 0))
    return pl.pallas_call(
        functools.partial(_mm_res_ln_kernel, alpha=alpha),
        grid=(m // tm,),
        in_specs=[rows(k), full((k, n)), rows(n), full((1, n)), full((1, n))],
        out_specs=rows(n),
        out_shape=jax.ShapeDtypeStruct((m, n), F32),
        compiler_params=_params("parallel"),
        name="mm_res_ln",
    )(x, w, res, g[None, :], b[None, :])


CARRY_ROWS = 8


def _conv_kernel(*refs, alpha, tm, chained):
    if chained:
        x_ref, win_ref, cw_ref, wout_ref, g_ref, b_ref, o_ref, u_ref, carry_s = refs
    else:
        x_ref, l0_ref, l1_ref, win_ref, cw_ref, wout_ref, g_ref, b_ref, o_ref, u_ref = refs
    d = x_ref.shape[1]
    x = x_ref[...]
    bch = jnp.dot(x.astype(BF16), win_ref[...], preferred_element_type=F32)
    b_g = bch[:, 0:d]
    u = bch[:, d:2 * d] * bch[:, 2 * d:3 * d]
    cw = cw_ref[...]
    if chained:
        i = pl.program_id(1)

        @pl.when(i == 0)
        def _zero_left():
            carry_s[...] = jnp.zeros(carry_s.shape, F32)

        prev = carry_s[...]
        row = lax.broadcasted_iota(jnp.int32, u.shape, 0)
        last = prev[CARRY_ROWS - 1:CARRY_ROWS]
        u1 = jnp.where(row == 0, last, pltpu.roll(u, 1, 0))
        u2 = jnp.where(row == 0, prev[CARRY_ROWS - 2:CARRY_ROWS - 1],
                       jnp.where(row == 1, last, pltpu.roll(u, 2, 0)))
        tail = u[tm - CARRY_ROWS:tm]
        carry_s[...] = tail
        u_ref[...] = tail
    else:
        u2 = l0_ref[...]
        u1 = l1_ref[...]
        u_ref[...] = u
    z = cw[0:1] * u2 + cw[1:2] * u1 + cw[2:3] * u
    y = jnp.dot((b_g * z).astype(BF16), wout_ref[...], preferred_element_type=F32)
    o_ref[...] = _layer_norm(alpha * x + y, g_ref[...], b_ref[...])


def conv_prompt(x, w_in, conv_w, w_out, g, b, *, batch, seq, alpha, tm):
    d = x.shape[1]
    nt = seq // tm
    rows = pl.BlockSpec((tm, d), lambda bi, i: (bi * nt + i, 0))
    full = lambda shape: pl.BlockSpec(shape, lambda bi, i: (0, 0))
    return pl.pallas_call(
        functools.partial(_conv_kernel, alpha=alpha, tm=tm, chained=True),
        grid=(batch, nt),
        in_specs=[rows, full((d, 3 * d)), full((CONV_W, d)), full((d, d)), full((1, d)), full((1, d))],
        out_specs=[rows, pl.BlockSpec((None, CARRY_ROWS, d), lambda bi, i: (bi, 0, 0))],
        out_shape=[jax.ShapeDtypeStruct((batch * seq, d), F32),
                   jax.ShapeDtypeStruct((batch, CARRY_ROWS, d), F32)],
        scratch_shapes=[pltpu.VMEM((CARRY_ROWS, d), F32)],
        compiler_params=_params("parallel", "arbitrary"),
        name="conv_prompt",
    )(x, w_in, conv_w, w_out, g[None, :], b[None, :])


def conv_sample(x, left0, left1, w_in, conv_w, w_out, g, b, *, alpha):
    m, d = x.shape
    full = lambda shape: pl.BlockSpec(shape, lambda i: (0, 0))
    return pl.pallas_call(
        functools.partial(_conv_kernel, alpha=alpha, tm=m, chained=False),
        grid=(1,),
        in_specs=[full((m, d)), full((m, d)), full((m, d)), full((d, 3 * d)), full((CONV_W, d)),
                  full((d, d)), full((1, d)), full((1, d))],
        out_specs=[full((m, d)), full((m, d))],
        out_shape=[jax.ShapeDtypeStruct((m, d), F32), jax.ShapeDtypeStruct((m, d), F32)],
        compiler_params=_params("arbitrary"),
        name="conv_sample",
    )(x, left0, left1, w_in, conv_w, w_out, g[None, :], b[None, :])


def _top_values(x, count, store):
    for r in range(count):
        mx = jnp.max(x, axis=0, keepdims=True)
        store(r, mx)
        if r + 1 < count:
            x = jnp.where(x == mx, -jnp.inf, x)


def _select_kernel(q_ref, keys_ref, thr_ref, e1_ref, s2_ref, e2_ref, s_s, sv_s):
    k = PEER_TOPK
    dh = keys_ref.shape[2]
    for hc in range(2 * PEER_HEADS):
        per_slab = q_ref.shape[2] // dh
        qb = q_ref[hc // per_slab, :, (hc % per_slab) * dh:(hc % per_slab + 1) * dh].astype(BF16)
        s = _nt_dot(keys_ref[hc], qb)
        s_s[hc] = s

        def store(r, mx, hc=hc):
            sv_s[hc, r:r + 1, :] = mx

        _top_values(s, k, store)

    for h in range(PEER_HEADS):
        sv1 = sv_s[2 * h]
        sv2 = sv_s[2 * h + 1]
        sub = lax.broadcasted_iota(jnp.int32, (SUBLANES, LANES), 0)
        pieces = [sv1[0:1] + sv2]
        for a in range(2, SUBLANES + 1):
            sums = sv1[a - 1:a] + sv2[0:SUBLANES]
            pieces.append(sums if k // a >= SUBLANES else jnp.where(sub < k // a, sums, -jnp.inf))
        pieces.append(sv1[SUBLANES:k] + sv2[0:1])
        cand = jnp.concatenate(pieces, axis=0)
        tau_box = []
        _top_values(cand, k, lambda r, mx: tau_box.append(mx))
        tau = tau_box[-1]
        top = sv1[0:1] + sv2[0:1]
        z = jnp.sum(jnp.where(cand >= tau, jnp.exp(cand - top), 0.0), axis=0, keepdims=True)
        s1 = s_s[2 * h]
        s2 = s_s[2 * h + 1]
        thr = jnp.full(s1.shape, jnp.inf, F32)
        for b in range(k // 2):
            sb = sv2[b:b + 1]
            thr = jnp.minimum(thr, jnp.where(s1 + sb >= tau, sb, jnp.inf))
        thr_best = jnp.full((1, LANES), jnp.inf, F32)
        for b in range(k // 2, k):
            sb = sv2[b:b + 1]
            thr_best = jnp.minimum(thr_best, jnp.where(sv1[0:1] + sb >= tau, sb, jnp.inf))
        thr = jnp.where(s1 == sv1[0:1], jnp.minimum(thr, thr_best), thr)
        thr_ref[0, h] = thr
        e1_ref[0, h] = jnp.exp(s1 - sv1[0:1]) * (1.0 / z)
        s2_ref[0, h] = s2
        e2_ref[0, h] = jnp.exp(s2 - sv2[0:1])


def peer_select(q_slabs, keys):
    n_slab, t, slab_w = q_slabs.shape
    nhc, _, dh = keys.shape
    nchunk = t // LANES
    out_sds = jax.ShapeDtypeStruct((nchunk, PEER_HEADS, N_KEYS, LANES), F32)
    out_spec = pl.BlockSpec((1, PEER_HEADS, N_KEYS, LANES), lambda i: (i, 0, 0, 0))
    return pl.pallas_call(
        _select_kernel,
        grid=(nchunk,),
        in_specs=[pl.BlockSpec((n_slab, LANES, slab_w), lambda i: (0, i, 0)),
                  pl.BlockSpec((nhc, N_KEYS, dh), lambda i: (0, 0, 0))],
        out_specs=[out_spec] * 4,
        out_shape=[out_sds] * 4,
        scratch_shapes=[pltpu.VMEM((nhc, N_KEYS, LANES), F32),
                        pltpu.VMEM((nhc, PEER_TOPK, LANES), F32)],
        compiler_params=_params("parallel"),
        name="peer_select",
    )(q_slabs, keys)


GATE_ROWS = 32
MXU_PIECES = 2
REGIONS_PER_HALF = 1


def _gelu(a):
    return 0.5 * a * (1.0 + lax.erf(a * (2.0 ** -0.5)))


def _peer_dense_kernel(xt_ref, u_ref, vt_ref, thr_ref, e1_ref, s2_ref, e2_ref, res_ref,
                       g_ref, b_ref, o_ref, a0_s, a1_s, w0_s, w1_s, acc_s, *, alpha, te, tt, n_tiles):
    gstep = pl.program_id(1)
    n_steps = pl.num_programs(1)
    n_i = te // N_KEYS

    @pl.when(gstep == 0)
    def _init():
        for ref in (a0_s, a1_s, w0_s, w1_s, acc_s):
            ref[...] = jnp.zeros(ref.shape, ref.dtype)

    assert n_i == 4 and tt % (2 * LANES) == 0
    th = tt // 2

    def gate_block(ii, tc, i0, a_ref, w_ref):
        lanes = slice(tc * LANES, (tc + 1) * LANES)
        thr = [thr_ref[tc, h, pl.ds(i0 + ii, 1), :] for h in range(PEER_HEADS)]
        e1 = [e1_ref[tc, h, pl.ds(i0 + ii, 1), :] for h in range(PEER_HEADS)]
        for rb in range(N_KEYS // GATE_ROWS):
            keys = slice(rb * GATE_ROWS, (rb + 1) * GATE_ROWS)
            gate = jnp.zeros((GATE_ROWS, LANES), F32)
            for h in range(PEER_HEADS):
                hit = s2_ref[tc, h, keys, :] >= thr[h]
                gate = gate + jnp.where(hit, e2_ref[tc, h, keys, :], 0.0) * e1[h]
            out_rows = slice(ii * N_KEYS + rb * GATE_ROWS, ii * N_KEYS + (rb + 1) * GATE_ROWS)
            w_ref[out_rows, lanes] = (gate * _gelu(a_ref[out_rows, lanes])).astype(BF16)

    def half_step(tile, a_src, w_dst, w_src, vt0, a_dst, u0, region_base):
        i0 = jnp.clip(tile, 0, n_tiles - 1) * n_i
        n_tc = tt // LANES
        def quarter(r):
            for piece in range(MXU_PIECES):
                if r < 2:
                    tok = slice(r * th, (r + 1) * th)
                    rows = slice(piece * (d // MXU_PIECES), (piece + 1) * (d // MXU_PIECES))
                    acc_s[rows, tok] += jnp.dot(vt_ref[rows, vt0:vt0 + te], w_src[:, tok],
                                                preferred_element_type=F32)
                else:
                    tok = slice((r - 2) * th, (r - 1) * th)
                    rows = slice(piece * (te // MXU_PIECES), (piece + 1) * (te // MXU_PIECES))
                    a_dst[rows, tok] = jnp.dot(u_ref[u0 + rows.start:u0 + rows.stop, :], xt_ref[:, tok],
                                               preferred_element_type=F32)
                for tc in range(piece * n_tc // MXU_PIECES, (piece + 1) * n_tc // MXU_PIECES):
                    gate_block(r, tc, i0, a_src, w_dst)

        per_region = n_i // REGIONS_PER_HALF
        for region in range(REGIONS_PER_HALF):
            @pl.when(gstep < n_steps + region_base + region)
            def _region(region=region):
                for r in range(region * per_region, (region + 1) * per_region):
                    quarter(r)

    d = acc_s.shape[0]
    half_step(2 * gstep - 1, a1_s, w1_s, w0_s, 0, a0_s, 0, 0)
    half_step(2 * gstep, a0_s, w0_s, w1_s, te, a1_s, te, REGIONS_PER_HALF)

    @pl.when(gstep == n_steps - 1)
    def _finish():
        y = acc_s[...].T
        o_ref[...] = _layer_norm(alpha * res_ref[...] + y, g_ref[...], b_ref[...])


def peer_dense(xt, u, vt, sel, res, g, b, *, alpha, tt, te):
    d, t = xt.shape
    n_exp = u.shape[0]
    n_tiles = n_exp // te
    assert n_tiles % 2 == 0
    n_pairs = n_tiles // 2
    nchunk = tt // LANES
    sel_spec = pl.BlockSpec((nchunk, PEER_HEADS, N_KEYS, LANES), lambda ti, s: (ti, 0, 0, 0))
    full = lambda shape: pl.BlockSpec(shape, lambda ti, s: (0, 0))
    u_spec = pl.BlockSpec((2 * te, d), lambda ti, s: (jnp.minimum(s, n_pairs - 1), 0))
    vt_spec = pl.BlockSpec((d, 2 * te), lambda ti, s: (0, jnp.maximum(s - 1, 0)))
    return pl.pallas_call(
        functools.partial(_peer_dense_kernel, alpha=alpha, te=te, tt=tt, n_tiles=n_tiles),
        grid=(t // tt, n_pairs + 1),
        in_specs=[pl.BlockSpec((d, tt), lambda ti, s: (0, ti)), u_spec, vt_spec,
                  sel_spec, sel_spec, sel_spec, sel_spec,
                  pl.BlockSpec((tt, d), lambda ti, s: (ti, 0)),
                  full((1, d)), full((1, d))],
        out_specs=pl.BlockSpec((tt, d), lambda ti, s: (ti, 0)),
        out_shape=jax.ShapeDtypeStruct((t, d), F32),
        scratch_shapes=[pltpu.VMEM((te, tt), F32), pltpu.VMEM((te, tt), F32),
                        pltpu.VMEM((te, tt), BF16), pltpu.VMEM((te, tt), BF16),
                        pltpu.VMEM((d, tt), F32)],
        compiler_params=_params("parallel", "arbitrary"),
        name="peer_dense",
    )(xt, u, vt, *sel, res, g[None, :], b[None, :])


def _ple_kernel(h_ref, p_ref, wg_ref, wp_ref, o_ref):
    h = h_ref[...]
    gate = jax.nn.sigmoid(jnp.dot(h.astype(BF16), wg_ref[...], preferred_element_type=F32))
    proj = jnp.dot(p_ref[...].astype(BF16), wp_ref[...], preferred_element_type=F32)
    o_ref[...] = h + gate * proj


def ple_add(h, p, wg, wp, *, tm):
    m, d = h.shape
    pd = p.shape[1]
    rows = lambda width: pl.BlockSpec((tm, width), lambda i: (i, 0))
    full = lambda shape: pl.BlockSpec(shape, lambda i: (0, 0))
    return pl.pallas_call(
        _ple_kernel,
        grid=(m // tm,),
        in_specs=[rows(d), rows(pd), full((d, d)), full((pd, d))],
        out_specs=rows(d),
        out_shape=jax.ShapeDtypeStruct((m, d), F32),
        compiler_params=_params("parallel"),
        name="ple_add",
    )(h, p, wg, wp)


TOKEN_TILE = 512
PEER_EXPERT_TILE = 512


def _largest_tile(n, candidates):
    for c in candidates:
        if n % c == 0:
            return c
    raise ValueError(f"no tile in {candidates} divides {n}")


def kernel(x_prompt, x_sample, cache_k, cache_v, state_conv, page_table, p_prompt, p_sample,
           ln_g, ln_b, w_attn_qkv, w_attn_lambda, attn_subln_g, w_attn_o,
           w_conv_in, conv_w, w_conv_out, w_peer_q, peer_keys, peer_u, peer_v,
           w_ple_gate, w_ple_proj):
    batch, seq, d = x_prompt.shape
    bs = x_sample.shape[0]
    assert x_sample.shape[1] == 1
    depth = ln_g.shape[0]
    tp = batch * seq
    t = tp + bs
    tm = TOKEN_TILE
    t_pad = -(-t // tm) * tm
    alpha = (2 * depth) ** 0.25
    width = ATT_HEADS * ATT_W
    tseq = _largest_tile(seq, (512, 256, 128))

    def joint(prompt_rows, sample_rows):
        pad = jnp.zeros((t_pad - t, prompt_rows.shape[1]), F32)
        return jnp.concatenate([prompt_rows, sample_rows, pad], axis=0)

    h = joint(x_prompt.reshape(tp, d), x_sample.reshape(bs, d))
    slopes = jnp.exp2(-8.0 * jnp.arange(1, ATT_HEADS + 1, dtype=F32) / ATT_HEADS)
    slopes_b = jnp.broadcast_to(slopes[:, None, None], (ATT_HEADS, 1, tseq))

    kp_l, vp_l, ks_l, vs_l, cp_l, cs_l = [], [], [], [], [], []
    for i in range(depth):
        j = i // N_MIXERS
        if i % N_MIXERS == 0:
            lam_init = 0.8 - 0.6 * math.exp(-0.3 * i)
            qkv = matmul_slabs(h, w_attn_qkv[j].astype(BF16), tm=tm, tn=width)
            g_sub = attn_subln_g[j]
            o_p = flash_prompt(qkv, slopes_b, w_attn_lambda[j], g_sub[None, :], batch=batch, seq=seq,
                               tq=tseq, tk=tseq, lam_init=lam_init)
            qkv_s = qkv[:, tp:t].reshape(3, bs, ATT_HEADS, ATT_W)
            o_s = decode_sample(page_table, qkv_s[0], qkv_s[1], qkv_s[2], cache_k, cache_v,
                                w_attn_lambda[j], g_sub, layer=j, lam_init=lam_init)
            o = joint(o_p, o_s.reshape(bs, width))
            h = mm_res_ln(o, w_attn_o[j].astype(BF16), h, ln_g[i, 0], ln_b[i, 0], alpha=alpha, tm=tm)
            kp_l.append(qkv[1, :tp].reshape(batch, seq, ATT_HEADS, ATT_W))
            vp_l.append(qkv[2, :tp].reshape(batch, seq, ATT_HEADS, ATT_W))
            ks_l.append(qkv_s[1].reshape(bs, 1, ATT_HEADS, ATT_W))
            vs_l.append(qkv_s[2].reshape(bs, 1, ATT_HEADS, ATT_W))
        else:
            w_in = w_conv_in[j].astype(BF16)
            w_out = w_conv_out[j].astype(BF16)
            hp, tail = conv_prompt(h[:tp], w_in, conv_w[j], w_out, ln_g[i, 0], ln_b[i, 0],
                                   batch=batch, seq=seq, alpha=alpha, tm=tseq)
            left = state_conv[j]
            hs, u_s = conv_sample(h[tp:t], left[:, 0], left[:, 1], w_in, conv_w[j], w_out,
                                  ln_g[i, 0], ln_b[i, 0], alpha=alpha)
            h = joint(hp, hs)
            cp_l.append(tail[:, CARRY_ROWS - (CONV_W - 1):])
            cs_l.append(jnp.stack([left[:, 1], u_s], axis=1))

        n_hc = 2 * PEER_HEADS
        dh = peer_keys.shape[-1]
        q_slabs = matmul_slabs(h, w_peer_q[i].astype(BF16), tm=tm, tn=d)
        keys = peer_keys[i].reshape(n_hc, N_KEYS, dh).astype(BF16)
        sel = peer_select(q_slabs, keys)
        h = peer_dense(h.T.astype(BF16), peer_u[i].astype(BF16), peer_v[i].T.astype(BF16), sel, h,
                       ln_g[i, 1], ln_b[i, 1], alpha=alpha, tt=tm, te=PEER_EXPERT_TILE)

        p = joint(p_prompt[i].reshape(tp, -1), p_sample[i].reshape(bs, -1))
        h = ple_add(h, p, w_ple_gate[i].astype(BF16), w_ple_proj[i].astype(BF16), tm=tm)

    return (h[:tp].reshape(batch, seq, d), h[tp:t].reshape(bs, 1, d),
            jnp.stack(kp_l), jnp.stack(vp_l), jnp.stack(ks_l), jnp.stack(vs_l),
            jnp.stack(cp_l), jnp.stack(cs_l))
```

```python
import functools
import math

import jax
import jax.numpy as jnp
from jax import lax
from jax.experimental import pallas as pl
from jax.experimental.pallas import tpu as pltpu

BF16 = jnp.bfloat16
F32 = jnp.float32

ATT_HEADS = 8
ATT_HD = 64
ATT_W = 2 * ATT_HD
N_MIXERS = 2
CONV_W = 3
PEER_HEADS = 8
N_KEYS = 128
PEER_TOPK = 16
LN_EPS = 1e-5
SUBLN_EPS = 1e-5
NEG_INF = -1e30
LANES = 128
SUBLANES = 8
VMEM_LIMIT = 56 * 1024 * 1024


def _params(*sem, flags=None):
    return pltpu.CompilerParams(dimension_semantics=sem, vmem_limit_bytes=VMEM_LIMIT, flags=flags)


def _nt_dot(a, b):
    return lax.dot_general(a, b, (((1,), (1,)), ((), ())), preferred_element_type=F32)


def _layer_norm(z, g, b):
    mu = jnp.mean(z, axis=-1, keepdims=True)
    zc = z - mu
    var = jnp.mean(zc * zc, axis=-1, keepdims=True)
    return zc * lax.rsqrt(var + LN_EPS) * g + b


def _diff_lambda(wl, lam_init):
    a = jnp.sum(wl[0:1] * wl[1:2], axis=1, keepdims=True)
    b = jnp.sum(wl[2:3] * wl[3:4], axis=1, keepdims=True)
    return jnp.exp(a) - jnp.exp(b) + lam_init


def _mm_kernel(x_ref, w_ref, o_ref):
    o_ref[...] = jnp.dot(x_ref[...].astype(BF16), w_ref[...], preferred_element_type=F32)


def matmul_slabs(x, w, *, tm, tn):
    m, k = x.shape
    n = w.shape[1]
    return pl.pallas_call(
        _mm_kernel,
        grid=(m // tm, n // tn),
        in_specs=[pl.BlockSpec((tm, k), lambda i, j: (i, 0)),
                  pl.BlockSpec((k, tn), lambda i, j: (0, j))],
        out_specs=pl.BlockSpec((None, tm, tn), lambda i, j: (j, i, 0)),
        out_shape=jax.ShapeDtypeStruct((n // tn, m, tn), F32),
        compiler_params=_params("parallel", "parallel"),
        name="matmul_slabs",
    )(x, w)


def _flash_kernel(slope_ref, wl_ref, g_ref, q_ref, k_ref, v_ref, o_ref,
                  q1_s, q2_s, m1_s, l1_s, a1_s, m2_s, l2_s, a2_s, *, tq, tk, lam_init):
    qi = pl.program_id(2)
    ki = pl.program_id(3)
    nk = pl.num_programs(3)
    q0 = qi * tq
    k0 = ki * tk

    @pl.when(ki == 0)
    def _init():
        q = q_ref[...] * (ATT_HD ** -0.5)
        lane = lax.broadcasted_iota(jnp.int32, q.shape, 1)
        q1_s[...] = jnp.where(lane < ATT_HD, q, 0.0).astype(BF16)
        q2_s[...] = jnp.where(lane >= ATT_HD, q, 0.0).astype(BF16)
        for m_s, l_s, a_s in ((m1_s, l1_s, a1_s), (m2_s, l2_s, a2_s)):
            m_s[...] = jnp.full(m_s.shape, NEG_INF, F32)
            l_s[...] = jnp.zeros(l_s.shape, F32)
            a_s[...] = jnp.zeros(a_s.shape, F32)

    def step(on_diagonal):
        kb = k_ref[...].astype(BF16)
        vb = v_ref[...].astype(BF16)
        col = lax.broadcasted_iota(jnp.int32, (1, tk), 1)
        bias = slope_ref[...] * (k0 + col - q0).astype(F32)
        if on_diagonal:
            rows = lax.broadcasted_iota(jnp.int32, (tq, tk), 0) + q0
            cols = lax.broadcasted_iota(jnp.int32, (tq, tk), 1) + k0
            future = cols > rows
        for q_s, m_s, l_s, a_s in ((q1_s, m1_s, l1_s, a1_s), (q2_s, m2_s, l2_s, a2_s)):
            s = _nt_dot(q_s[...], kb) + bias
            if on_diagonal:
                s = jnp.where(future, NEG_INF, s)
            m_old = m_s[...]
            m_new = jnp.maximum(m_old, jnp.max(s, axis=1, keepdims=True))
            alpha = jnp.exp(m_old - m_new)
            p = jnp.exp(s - jnp.tile(m_new, (1, tk // LANES)))
            l_s[...] = alpha * l_s[...] + jnp.sum(p, axis=1, keepdims=True)
            a_s[...] = alpha * a_s[...] + jnp.dot(p.astype(BF16), vb, preferred_element_type=F32)
            m_s[...] = m_new

    pl.when(ki < qi)(functools.partial(step, False))
    pl.when(ki == qi)(functools.partial(step, True))

    @pl.when(ki == nk - 1)
    def _finish():
        lam = _diff_lambda(wl_ref[...], lam_init)
        o = a1_s[...] / l1_s[...] - lam * (a2_s[...] / l2_s[...])
        ms = jnp.mean(o * o, axis=1, keepdims=True)
        o_ref[...] = o * lax.rsqrt(ms + SUBLN_EPS) * g_ref[...] * (1.0 - lam_init)


def flash_prompt(qkv, slopes, w_lam, subln_g, *, batch, seq, tq, tk, lam_init):
    assert tq == tk
    nq = seq // tq
    nkb = seq // tk

    def kv_row(b, qi, ki):
        return b * nkb + jnp.minimum(ki, qi)

    kern = functools.partial(_flash_kernel, tq=tq, tk=tk, lam_init=lam_init)
    return pl.pallas_call(
        kern,
        grid=(batch, ATT_HEADS, nq, nkb),
        in_specs=[
            pl.BlockSpec((None, 1, tk), lambda b, h, qi, ki: (h, 0, 0)),
            pl.BlockSpec((4, ATT_HD), lambda b, h, qi, ki: (0, 0)),
            pl.BlockSpec((1, ATT_W), lambda b, h, qi, ki: (0, 0)),
            pl.BlockSpec((None, tq, ATT_W), lambda b, h, qi, ki: (0, b * nq + qi, h)),
            pl.BlockSpec((None, tk, ATT_W), lambda b, h, qi, ki: (1, kv_row(b, qi, ki), h)),
            pl.BlockSpec((None, tk, ATT_W), lambda b, h, qi, ki: (2, kv_row(b, qi, ki), h)),
        ],
        out_specs=pl.BlockSpec((tq, ATT_W), lambda b, h, qi, ki: (b * nq + qi, h)),
        out_shape=jax.ShapeDtypeStruct((batch * seq, ATT_HEADS * ATT_W), F32),
        scratch_shapes=[
            pltpu.VMEM((tq, ATT_W), BF16), pltpu.VMEM((tq, ATT_W), BF16),
            pltpu.VMEM((tq, LANES), F32), pltpu.VMEM((tq, LANES), F32), pltpu.VMEM((tq, ATT_W), F32),
            pltpu.VMEM((tq, LANES), F32), pltpu.VMEM((tq, LANES), F32), pltpu.VMEM((tq, ATT_W), F32),
        ],
        compiler_params=_params("parallel", "parallel", "parallel", "arbitrary"),
        name="flash_prompt",
    )(slopes, w_lam, subln_g, qkv, qkv, qkv)


DECODE_PAGES_PER_STEP = 4


def _decode_kernel(pt_ref, q_ref, kn_ref, vn_ref, *rest, n_grp, page, past, lam_init):
    del pt_ref
    k_refs = rest[:n_grp]
    v_refs = rest[n_grp:2 * n_grp]
    bsel_ref, slope_ref, alibi_ref, g_ref, wl_ref, o_ref, m_s, l_s, a1_s, a2_s = rest[2 * n_grp:]
    pg = pl.program_id(1)
    n_steps = pl.num_programs(1)
    nh = ATT_HEADS
    q8 = q_ref[...] * (ATT_HD ** -0.5)

    def half_sums(prod):
        return jnp.dot(prod.astype(BF16), bsel_ref[...], preferred_element_type=F32)

    @pl.when(pg == 0)
    def _init():
        m_s[...] = jnp.full(m_s.shape, NEG_INF, F32)
        l_s[...] = jnp.zeros(l_s.shape, F32)
        a1_s[...] = jnp.zeros(a1_s.shape, F32)
        a2_s[...] = jnp.zeros(a2_s.shape, F32)

    slope = slope_ref[...]
    logits, shifts = [], []
    m_new = m_s[...]
    for g in range(n_grp):
        prod = (k_refs[g][...] * q8[None]).reshape(page * nh, ATT_W)
        s3 = half_sums(prod).reshape(page, nh, 2 * ATT_W) + alibi_ref[...]
        shift = slope * (past - (pg * n_grp + g) * page).astype(F32)
        m_new = jnp.maximum(m_new, jnp.max(s3, axis=0) - shift)
        logits.append(s3)
        shifts.append(shift)
    alpha = jnp.exp(m_s[...] - m_new)
    l = alpha * l_s[...]
    acc1 = alpha[:, :ATT_W] * a1_s[...]
    acc2 = alpha[:, ATT_W:] * a2_s[...]
    for g in range(n_grp):
        pe = jnp.exp(logits[g] - (m_new + shifts[g])[None])
        v3 = v_refs[g][...]
        l = l + jnp.sum(pe, axis=0)
        acc1 = acc1 + jnp.sum(pe[:, :, :ATT_W] * v3, axis=0)
        acc2 = acc2 + jnp.sum(pe[:, :, ATT_W:] * v3, axis=0)
    m_s[...] = m_new
    l_s[...] = l
    a1_s[...] = acc1
    a2_s[...] = acc2

    @pl.when(pg == n_steps - 1)
    def _finish():
        s_self = half_sums(q8 * kn_ref[...])
        m_n = jnp.maximum(m_new, s_self)
        al = jnp.exp(m_new - m_n)
        p_self = jnp.exp(s_self - m_n)
        lf = al * l + p_self
        vn = vn_ref[...]
        o1 = (al[:, :ATT_W] * acc1 + p_self[:, :ATT_W] * vn) / lf[:, :ATT_W]
        o2 = (al[:, ATT_W:] * acc2 + p_self[:, ATT_W:] * vn) / lf[:, ATT_W:]
        lam = _diff_lambda(wl_ref[...], lam_init)
        d = o1 - lam * o2
        ms = jnp.mean(d * d, axis=1, keepdims=True)
        o_ref[...] = d * lax.rsqrt(ms + SUBLN_EPS) * g_ref[...] * (1.0 - lam_init)


def decode_sample(page_table, q, k_new, v_new, cache_k, cache_v, w_lam, subln_g, *, layer, lam_init):
    bs, n_pages = page_table.shape
    page = cache_k.shape[2]
    nh = ATT_HEADS
    past = n_pages * page
    n_grp = DECODE_PAGES_PER_STEP
    assert n_pages % n_grp == 0
    half = jnp.arange(ATT_W)[:, None] // ATT_HD == jnp.arange(2 * ATT_W)[None, :] // ATT_W
    bsel = half.astype(BF16)
    slopes = jnp.exp2(-8.0 * jnp.arange(1, nh + 1, dtype=F32) / nh)
    slope = jnp.broadcast_to(slopes[:, None], (nh, 2 * ATT_W))
    alibi = jnp.arange(page, dtype=F32)[:, None, None] * slope[None]

    row_spec = pl.BlockSpec((None, nh, ATT_W), lambda b, p, pt: (b, 0, 0))
    const = lambda shape: pl.BlockSpec(shape, lambda b, p, pt: (0,) * len(shape))

    def page_spec(g):
        return pl.BlockSpec((None, None, page, nh, ATT_W),
                            lambda b, p, pt: (layer, pt[b, p * n_grp + g], 0, 0, 0))

    pages = [page_spec(g) for g in range(n_grp)]
    kern = functools.partial(_decode_kernel, n_grp=n_grp, page=page, past=past, lam_init=lam_init)
    return pl.pallas_call(
        kern,
        grid_spec=pltpu.PrefetchScalarGridSpec(
            num_scalar_prefetch=1,
            grid=(bs, n_pages // n_grp),
            in_specs=[row_spec, row_spec, row_spec] + pages + pages + [
                const((ATT_W, 2 * ATT_W)), const((nh, 2 * ATT_W)), const((page, nh, 2 * ATT_W)),
                const((1, ATT_W)), const((4, ATT_HD))],
            out_specs=pl.BlockSpec((None, nh, ATT_W), lambda b, p, pt: (b, 0, 0)),
            scratch_shapes=[pltpu.VMEM((nh, 2 * ATT_W), F32), pltpu.VMEM((nh, 2 * ATT_W), F32),
                            pltpu.VMEM((nh, ATT_W), F32), pltpu.VMEM((nh, ATT_W), F32)],
        ),
        out_shape=jax.ShapeDtypeStruct((bs, nh, ATT_W), F32),
        compiler_params=_params("parallel", "arbitrary"),
        name="decode_sample",
    )(page_table, q, k_new, v_new, *([cache_k] * n_grp), *([cache_v] * n_grp),
      bsel, slope, alibi, subln_g[None, :], w_lam)


def _mm_res_ln_kernel(x_ref, w_ref, res_ref, g_ref, b_ref, o_ref, *, alpha):
    y = jnp.dot(x_ref[...].astype(BF16), w_ref[...], preferred_element_type=F32)
    o_ref[...] = _layer_norm(alpha * res_ref[...] + y, g_ref[...], b_ref[...])


def mm_res_ln(x, w, res, g, b, *, alpha, tm):
    m, k = x.shape
    n = w.shape[1]
    rows = lambda width: pl.BlockSpec((tm, width), lambda i: (i, 0))
    full = lambda shape: pl.BlockSpec(shape, lambda i: (0, 0))
    return pl.pallas_call(
        functools.partial(_mm_res_ln_kernel, alpha=alpha),
        grid=(m // tm,),
        in_specs=[rows(k), full((k, n)), rows(n), full((1, n)), full((1, n))],
        out_specs=rows(n),
        out_shape=jax.ShapeDtypeStruct((m, n), F32),
        compiler_params=_params("parallel"),
        name="mm_res_ln",
    )(x, w, res, g[None, :], b[None, :])


CARRY_ROWS = 8


def _conv_kernel(*refs, alpha, tm, chained):
    if chained:
        x_ref, win_ref, cw_ref, wout_ref, g_ref, b_ref, o_ref, u_ref, carry_s = refs
    else:
        x_ref, l0_ref, l1_ref, win_ref, cw_ref, wout_ref, g_ref, b_ref, o_ref, u_ref = refs
    d = x_ref.shape[1]
    x = x_ref[...]
    bch = jnp.dot(x.astype(BF16), win_ref[...], preferred_element_type=F32)
    b_g = bch[:, 0:d]
    u = bch[:, d:2 * d] * bch[:, 2 * d:3 * d]
    cw = cw_ref[...]
    if chained:
        i = pl.program_id(1)

        @pl.when(i == 0)
        def _zero_left():
            carry_s[...] = jnp.zeros(carry_s.shape, F32)

        prev = carry_s[...]
        row = lax.broadcasted_iota(jnp.int32, u.shape, 0)
        last = prev[CARRY_ROWS - 1:CARRY_ROWS]
        u1 = jnp.where(row == 0, last, pltpu.roll(u, 1, 0))
        u2 = jnp.where(row == 0, prev[CARRY_ROWS - 2:CARRY_ROWS - 1],
                       jnp.where(row == 1, last, pltpu.roll(u, 2, 0)))
        tail = u[tm - CARRY_ROWS:tm]
        carry_s[...] = tail
        u_ref[...] = tail
    else:
        u2 = l0_ref[...]
        u1 = l1_ref[...]
        u_ref[...] = u
    z = cw[0:1] * u2 + cw[1:2] * u1 + cw[2:3] * u
    y = jnp.dot((b_g * z).astype(BF16), wout_ref[...], preferred_element_type=F32)
    o_ref[...] = _layer_norm(alpha * x + y, g_ref[...], b_ref[...])


def conv_prompt(x, w_in, conv_w, w_out, g, b, *, batch, seq, alpha, tm):
    d = x.shape[1]
    nt = seq // tm
    rows = pl.BlockSpec((tm, d), lambda bi, i: (bi * nt + i, 0))
    full = lambda shape: pl.BlockSpec(shape, lambda bi, i: (0, 0))
    return pl.pallas_call(
        functools.partial(_conv_kernel, alpha=alpha, tm=tm, chained=True),
        grid=(batch, nt),
        in_specs=[rows, full((d, 3 * d)), full((CONV_W, d)), full((d, d)), full((1, d)), full((1, d))],
        out_specs=[rows, pl.BlockSpec((None, CARRY_ROWS, d), lambda bi, i: (bi, 0, 0))],
        out_shape=[jax.ShapeDtypeStruct((batch * seq, d), F32),
                   jax.ShapeDtypeStruct((batch, CARRY_ROWS, d), F32)],
        scratch_shapes=[pltpu.VMEM((CARRY_ROWS, d), F32)],
        compiler_params=_params("parallel", "arbitrary"),
        name="conv_prompt",
    )(x, w_in, conv_w, w_out, g[None, :], b[None, :])


def conv_sample(x, left0, left1, w_in, conv_w, w_out, g, b, *, alpha):
    m, d = x.shape
    full = lambda shape: pl.BlockSpec(shape, lambda i: (0, 0))
    return pl.pallas_call(
        functools.partial(_conv_kernel, alpha=alpha, tm=m, chained=False),
        grid=(1,),
        in_specs=[full((m, d)), full((m, d)), full((m, d)), full((d, 3 * d)), full((CONV_W, d)),
                  full((d, d)), full((1, d)), full((1, d))],
        out_specs=[full((m, d)), full((m, d))],
        out_shape=[jax.ShapeDtypeStruct((m, d), F32), jax.ShapeDtypeStruct((m, d), F32)],
        compiler_params=_params("arbitrary"),
        name="conv_sample",
    )(x, left0, left1, w_in, conv_w, w_out, g[None, :], b[None, :])


PACK = 4 // jnp.dtype(BF16).itemsize


def _pack_rows(x):
    return pltpu.bitcast(x.astype(BF16), jnp.uint32)


def _unpack_rows(x):
    return pltpu.bitcast(x, BF16)


def _replicate_word(x):
    if PACK == 1:
        return pltpu.bitcast(x, jnp.uint32)
    hi = pltpu.bitcast(x.astype(BF16).astype(F32), jnp.uint32)
    return hi | (hi >> 16)


def _top_values(x, count, store, want_rank=False):
    rank = jnp.full(x.shape, float(count), F32) if want_rank else None
    for r in range(count):
        mx = jnp.max(x, axis=0, keepdims=True)
        store(r, mx)
        hit = x == mx
        if want_rank:
            rank = jnp.where(hit, float(r), rank)
        if r + 1 < count:
            x = jnp.where(hit, -jnp.inf, x)
    return rank


def _select_kernel(q_ref, keys_ref, cnt_ref, e1_ref, r2_ref, e2_ref, s1_s, sv_s):
    k = PEER_TOPK
    dh = keys_ref.shape[2]
    per_slab = q_ref.shape[2] // dh
    for hc in range(2 * PEER_HEADS):
        h, second = divmod(hc, 2)
        qb = q_ref[hc // per_slab, :, (hc % per_slab) * dh:(hc % per_slab + 1) * dh].astype(BF16)
        s = _nt_dot(keys_ref[hc], qb)

        def store(r, mx, hc=hc):
            sv_s[hc, r:r + 1, :] = mx

        rank = _top_values(s, k, store, want_rank=bool(second))
        if second:
            r2_ref[0, h] = _pack_rows(rank)
            e2_ref[0, h] = _pack_rows(jnp.exp(s - sv_s[hc, 0:1, :]))
        else:
            s1_s[h] = s

    for h in range(PEER_HEADS):
        sv1 = sv_s[2 * h]
        sv2 = sv_s[2 * h + 1]
        sub = lax.broadcasted_iota(jnp.int32, (SUBLANES, LANES), 0)
        pieces = [sv1[0:1] + sv2]
        for a in range(2, SUBLANES + 1):
            sums = sv1[a - 1:a] + sv2[0:SUBLANES]
            pieces.append(sums if k // a >= SUBLANES else jnp.where(sub < k // a, sums, -jnp.inf))
        pieces.append(sv1[SUBLANES:k] + sv2[0:1])
        cand = jnp.concatenate(pieces, axis=0)
        tau_box = []
        _top_values(cand, k, lambda r, mx: tau_box.append(mx))
        tau = tau_box[-1]
        top = sv1[0:1] + sv2[0:1]
        z = jnp.sum(jnp.where(cand >= tau, jnp.exp(cand - top), 0.0), axis=0, keepdims=True)
        s1 = s1_s[h]
        cnt = jnp.zeros(s1.shape, F32)
        for b in range(k // 2):
            cnt = cnt + jnp.where(s1 + sv2[b:b + 1] >= tau, 1.0, 0.0)
        cnt_best = jnp.zeros((1, LANES), F32)
        for b in range(k // 2, k):
            cnt_best = cnt_best + jnp.where(sv1[0:1] + sv2[b:b + 1] >= tau, 1.0, 0.0)
        cnt_ref[0, h] = _replicate_word(jnp.where(s1 == sv1[0:1], cnt + cnt_best, cnt))
        e1_ref[0, h] = _replicate_word(jnp.exp(s1 - sv1[0:1]) * (1.0 / z))


def peer_select(q_slabs, keys):
    n_slab, t, slab_w = q_slabs.shape
    nhc, _, dh = keys.shape
    nchunk = t // LANES
    def out(rows, dtype):
        spec = pl.BlockSpec((1, PEER_HEADS, rows, LANES), lambda i: (i, 0, 0, 0))
        return spec, jax.ShapeDtypeStruct((nchunk, PEER_HEADS, rows, LANES), dtype)

    outs = [out(N_KEYS, jnp.uint32), out(N_KEYS, jnp.uint32),
            out(N_KEYS // PACK, jnp.uint32), out(N_KEYS // PACK, jnp.uint32)]
    return pl.pallas_call(
        _select_kernel,
        grid=(nchunk,),
        in_specs=[pl.BlockSpec((n_slab, LANES, slab_w), lambda i: (0, i, 0)),
                  pl.BlockSpec((nhc, N_KEYS, dh), lambda i: (0, 0, 0))],
        out_specs=[spec for spec, _ in outs],
        out_shape=[sds for _, sds in outs],
        scratch_shapes=[pltpu.VMEM((PEER_HEADS, N_KEYS, LANES), F32),
                        pltpu.VMEM((nhc, PEER_TOPK, LANES), F32)],
        compiler_params=_params("parallel"),
        name="peer_select",
    )(q_slabs, keys)


GATE_ROWS = 16
MXU_PIECES = 4
REGIONS_PER_HALF = 1


def _gelu(a):
    return 0.5 * a * (1.0 + lax.erf(a * (2.0 ** -0.5)))


def _peer_dense_kernel(xt_ref, u_ref, vt_ref, cnt_ref, e1_ref, r2_ref, e2_ref, res_ref,
                       g_ref, b_ref, o_ref, a0_s, a1_s, w0_s, w1_s, acc_s, *, alpha, te, tt, n_tiles):
    gstep = pl.program_id(1)
    n_steps = pl.num_programs(1)
    n_i = te // N_KEYS

    @pl.when(gstep == 0)
    def _init():
        for ref in (a0_s, a1_s, w0_s, w1_s, acc_s):
            ref[...] = jnp.zeros(ref.shape, ref.dtype)

    assert n_i == 4 and tt % (2 * LANES) == 0
    th = tt // 2

    def gate_block(iis, tc, rbs, i0, a_ref, w_ref):
        lanes = slice(tc * LANES, (tc + 1) * LANES)

        def row(ref, ii, h):
            r = ref[tc, h, pl.ds(i0 + ii, 1), :]
            return _unpack_rows(jnp.broadcast_to(r, (GATE_ROWS // PACK, LANES)))

        cnt = {(ii, h): row(cnt_ref, ii, h) for ii in iis for h in range(PEER_HEADS)}
        e1 = {(ii, h): row(e1_ref, ii, h) for ii in iis for h in range(PEER_HEADS)}
        zero = jnp.zeros((GATE_ROWS, LANES), BF16)
        for rb in rbs:
            keys = slice(rb * GATE_ROWS // PACK, (rb + 1) * GATE_ROWS // PACK)
            gate = {ii: zero for ii in iis}
            for h in range(PEER_HEADS):
                r2 = _unpack_rows(r2_ref[tc, h, keys, :])
                e2 = _unpack_rows(e2_ref[tc, h, keys, :])
                for ii in iis:
                    gate[ii] = gate[ii] + jnp.where(r2 < cnt[ii, h], e2, zero) * e1[ii, h]
            for ii in iis:
                out_rows = slice(ii * N_KEYS + rb * GATE_ROWS, ii * N_KEYS + (rb + 1) * GATE_ROWS)
                w_ref[out_rows, lanes] = gate[ii] * _gelu(a_ref[out_rows, lanes].astype(BF16))

    def half_step(tile, a_src, w_dst, w_src, vt0, a_dst, u0, region_base):
        i0 = jnp.clip(tile, 0, n_tiles - 1) * n_i
        n_tc = tt // LANES
        def quarter(r):
            iis = (2 * (r // 2), 2 * (r // 2) + 1)
            n_rb = N_KEYS // GATE_ROWS
            units = [(tc, range(part * n_rb // 2, (part + 1) * n_rb // 2))
                     for tc in range((r % 2) * n_tc // 2, (r % 2 + 1) * n_tc // 2) for part in range(2)]
            for piece in range(MXU_PIECES):
                if r < 2:
                    tok = slice(r * th, (r + 1) * th)
                    rows = slice(piece * (d // MXU_PIECES), (piece + 1) * (d // MXU_PIECES))
                    acc_s[rows, tok] += jnp.dot(vt_ref[rows, vt0:vt0 + te], w_src[:, tok],
                                                preferred_element_type=F32)
                else:
                    tok = slice((r - 2) * th, (r - 1) * th)
                    rows = slice(piece * (te // MXU_PIECES), (piece + 1) * (te // MXU_PIECES))
                    a_dst[rows, tok] = jnp.dot(u_ref[u0 + rows.start:u0 + rows.stop, :], xt_ref[:, tok],
                                               preferred_element_type=F32)
                for tc, rbs in units[piece * len(units) // MXU_PIECES:(piece + 1) * len(units) // MXU_PIECES]:
                    gate_block(iis, tc, rbs, i0, a_src, w_dst)

        per_region = n_i // REGIONS_PER_HALF
        for region in range(REGIONS_PER_HALF):
            @pl.when(gstep < n_steps + region_base + region)
            def _region(region=region):
                for r in range(region * per_region, (region + 1) * per_region):
                    quarter(r)

    d = acc_s.shape[0]
    half_step(2 * gstep - 1, a1_s, w1_s, w0_s, 0, a0_s, 0, 0)
    half_step(2 * gstep, a0_s, w0_s, w1_s, te, a1_s, te, REGIONS_PER_HALF)

    @pl.when(gstep == n_steps - 1)
    def _finish():
        y = acc_s[...].T
        o_ref[...] = _layer_norm(alpha * res_ref[...] + y, g_ref[...], b_ref[...])


def peer_dense(xt, u, vt, sel, res, g, b, *, alpha, tt, te):
    d, t = xt.shape
    n_exp = u.shape[0]
    n_tiles = n_exp // te
    assert n_tiles % 2 == 0
    n_pairs = n_tiles // 2
    nchunk = tt // LANES
    sel_specs = [pl.BlockSpec((nchunk,) + a.shape[1:], lambda ti, s: (ti, 0, 0, 0)) for a in sel]
    full = lambda shape: pl.BlockSpec(shape, lambda ti, s: (0, 0))
    u_spec = pl.BlockSpec((2 * te, d), lambda ti, s: (jnp.minimum(s, n_pairs - 1), 0))
    vt_spec = pl.BlockSpec((d, 2 * te), lambda ti, s: (0, jnp.maximum(s - 1, 0)))
    return pl.pallas_call(
        functools.partial(_peer_dense_kernel, alpha=alpha, te=te, tt=tt, n_tiles=n_tiles),
        grid=(t // tt, n_pairs + 1),
        in_specs=[pl.BlockSpec((d, tt), lambda ti, s: (0, ti)), u_spec, vt_spec,
                  *sel_specs,
                  pl.BlockSpec((tt, d), lambda ti, s: (ti, 0)),
                  full((1, d)), full((1, d))],
        out_specs=pl.BlockSpec((tt, d), lambda ti, s: (ti, 0)),
        out_shape=jax.ShapeDtypeStruct((t, d), F32),
        scratch_shapes=[pltpu.VMEM((te, tt), F32), pltpu.VMEM((te, tt), F32),
                        pltpu.VMEM((te, tt), BF16), pltpu.VMEM((te, tt), BF16),
                        pltpu.VMEM((d, tt), F32)],
        compiler_params=_params("parallel", "arbitrary"),
        name="peer_dense",
    )(xt, u, vt, *sel, res, g[None, :], b[None, :])


def _ple_kernel(h_ref, p_ref, wg_ref, wp_ref, o_ref):
    h = h_ref[...]
    gate = jax.nn.sigmoid(jnp.dot(h.astype(BF16), wg_ref[...], preferred_element_type=F32))
    proj = jnp.dot(p_ref[...].astype(BF16), wp_ref[...], preferred_element_type=F32)
    o_ref[...] = h + gate * proj


def ple_add(h, p, wg, wp, *, tm):
    m, d = h.shape
    pd = p.shape[1]
    rows = lambda width: pl.BlockSpec((tm, width), lambda i: (i, 0))
    full = lambda shape: pl.BlockSpec(shape, lambda i: (0, 0))
    return pl.pallas_call(
        _ple_kernel,
        grid=(m // tm,),
        in_specs=[rows(d), rows(pd), full((d, d)), full((pd, d))],
        out_specs=rows(d),
        out_shape=jax.ShapeDtypeStruct((m, d), F32),
        compiler_params=_params("parallel"),
        name="ple_add",
    )(h, p, wg, wp)


TOKEN_TILE = 512
PEER_EXPERT_TILE = 512


def _largest_tile(n, candidates):
    for c in candidates:
        if n % c == 0:
            return c
    raise ValueError(f"no tile in {candidates} divides {n}")


def kernel(x_prompt, x_sample, cache_k, cache_v, state_conv, page_table, p_prompt, p_sample,
           ln_g, ln_b, w_attn_qkv, w_attn_lambda, attn_subln_g, w_attn_o,
           w_conv_in, conv_w, w_conv_out, w_peer_q, peer_keys, peer_u, peer_v,
           w_ple_gate, w_ple_proj):
    batch, seq, d = x_prompt.shape
    bs = x_sample.shape[0]
    assert x_sample.shape[1] == 1
    depth = ln_g.shape[0]
    tp = batch * seq
    t = tp + bs
    tm = TOKEN_TILE
    t_pad = -(-t // tm) * tm
    alpha = (2 * depth) ** 0.25
    width = ATT_HEADS * ATT_W
    tseq = _largest_tile(seq, (512, 256, 128))

    def joint(prompt_rows, sample_rows):
        pad = jnp.zeros((t_pad - t, prompt_rows.shape[1]), F32)
        return jnp.concatenate([prompt_rows, sample_rows, pad], axis=0)

    h = joint(x_prompt.reshape(tp, d), x_sample.reshape(bs, d))
    slopes = jnp.exp2(-8.0 * jnp.arange(1, ATT_HEADS + 1, dtype=F32) / ATT_HEADS)
    slopes_b = jnp.broadcast_to(slopes[:, None, None], (ATT_HEADS, 1, tseq))

    kp_l, vp_l, ks_l, vs_l, cp_l, cs_l = [], [], [], [], [], []
    for i in range(depth):
        j = i // N_MIXERS
        if i % N_MIXERS == 0:
            lam_init = 0.8 - 0.6 * math.exp(-0.3 * i)
            qkv = matmul_slabs(h, w_attn_qkv[j].astype(BF16), tm=tm, tn=width)
            g_sub = attn_subln_g[j]
            o_p = flash_prompt(qkv, slopes_b, w_attn_lambda[j], g_sub[None, :], batch=batch, seq=seq,
                               tq=tseq, tk=tseq, lam_init=lam_init)
            qkv_s = qkv[:, tp:t].reshape(3, bs, ATT_HEADS, ATT_W)
            o_s = decode_sample(page_table, qkv_s[0], qkv_s[1], qkv_s[2], cache_k, cache_v,
                                w_attn_lambda[j], g_sub, layer=j, lam_init=lam_init)
            o = joint(o_p, o_s.reshape(bs, width))
            h = mm_res_ln(o, w_attn_o[j].astype(BF16), h, ln_g[i, 0], ln_b[i, 0], alpha=alpha, tm=tm)
            kp_l.append(qkv[1, :tp].reshape(batch, seq, ATT_HEADS, ATT_W))
            vp_l.append(qkv[2, :tp].reshape(batch, seq, ATT_HEADS, ATT_W))
            ks_l.append(qkv_s[1].reshape(bs, 1, ATT_HEADS, ATT_W))
            vs_l.append(qkv_s[2].reshape(bs, 1, ATT_HEADS, ATT_W))
        else:
            w_in = w_conv_in[j].astype(BF16)
            w_out = w_conv_out[j].astype(BF16)
            hp, tail = conv_prompt(h[:tp], w_in, conv_w[j], w_out, ln_g[i, 0], ln_b[i, 0],
                                   batch=batch, seq=seq, alpha=alpha, tm=tseq)
            left = state_conv[j]
            hs, u_s = conv_sample(h[tp:t], left[:, 0], left[:, 1], w_in, conv_w[j], w_out,
                                  ln_g[i, 0], ln_b[i, 0], alpha=alpha)
            h = joint(hp, hs)
            cp_l.append(tail[:, CARRY_ROWS - (CONV_W - 1):])
            cs_l.append(jnp.stack([left[:, 1], u_s], axis=1))

        n_hc = 2 * PEER_HEADS
        dh = peer_keys.shape[-1]
        q_slabs = matmul_slabs(h, w_peer_q[i].astype(BF16), tm=tm, tn=d)
        keys = peer_keys[i].reshape(n_hc, N_KEYS, dh).astype(BF16)
        sel = peer_select(q_slabs, keys)
        h = peer_dense(h.T.astype(BF16), peer_u[i].astype(BF16), peer_v[i].T.astype(BF16), sel, h,
                       ln_g[i, 1], ln_b[i, 1], alpha=alpha, tt=tm, te=PEER_EXPERT_TILE)

        p = joint(p_prompt[i].reshape(tp, -1), p_sample[i].reshape(bs, -1))
        h = ple_add(h, p, w_ple_gate[i].astype(BF16), w_ple_proj[i].astype(BF16), tm=tm)

    return (h[:tp].reshape(batch, seq, d), h[tp:t].reshape(bs, 1, d),
            jnp.stack(kp_l), jnp.stack(vp_l), jnp.stack(ks_l), jnp.stack(vs_l),
            jnp.stack(cp_l), jnp.stack(cs_l))
```

```python
import functools
import math

import jax
import jax.numpy as jnp
from jax import lax
from jax.experimental import pallas as pl
from jax.experimental.pallas import tpu as pltpu

BF16 = jnp.bfloat16
F32 = jnp.float32

ATT_HEADS = 8
ATT_HD = 64
ATT_W = 2 * ATT_HD
N_MIXERS = 2
CONV_W = 3
PEER_HEADS = 8
N_KEYS = 128
PEER_TOPK = 16
LN_EPS = 1e-5
SUBLN_EPS = 1e-5
NEG_INF = -1e30
LOG2E = math.log2(math.e)
LANES = 128
SUBLANES = 8
VMEM_LIMIT = 56 * 1024 * 1024


def _params(*sem, flags=None):
    return pltpu.CompilerParams(dimension_semantics=sem, vmem_limit_bytes=VMEM_LIMIT, flags=flags)


def _nt_dot(a, b):
    return lax.dot_general(a, b, (((1,), (1,)), ((), ())), preferred_element_type=F32)


def _layer_norm(z, g, b):
    mu = jnp.mean(z, axis=-1, keepdims=True)
    zc = z - mu
    var = jnp.mean(zc * zc, axis=-1, keepdims=True)
    return zc * lax.rsqrt(var + LN_EPS) * g + b


def _diff_lambda(wl, lam_init):
    a = jnp.sum(wl[0:1] * wl[1:2], axis=1, keepdims=True)
    b = jnp.sum(wl[2:3] * wl[3:4], axis=1, keepdims=True)
    return jnp.exp(a) - jnp.exp(b) + lam_init


def _mm_kernel(x_ref, w_ref, o_ref):
    o_ref[...] = jnp.dot(x_ref[...].astype(BF16), w_ref[...], preferred_element_type=F32)


def matmul_slabs(x, w, *, tm, tn):
    m, k = x.shape
    n = w.shape[1]
    return pl.pallas_call(
        _mm_kernel,
        grid=(m // tm, n // tn),
        in_specs=[pl.BlockSpec((tm, k), lambda i, j: (i, 0)),
                  pl.BlockSpec((k, tn), lambda i, j: (0, j))],
        out_specs=pl.BlockSpec((None, tm, tn), lambda i, j: (j, i, 0)),
        out_shape=jax.ShapeDtypeStruct((n // tn, m, tn), F32),
        compiler_params=_params("parallel", "parallel"),
        name="matmul_slabs",
    )(x, w)


FLASH_ROW_GROUPS = 4


def _flash_kernel(slope_ref, wl_ref, g_ref, q_ref, k_ref, v_ref, o_ref,
                  q_s, m_s, l_s, a_s, *, tq, tk, lam_init):
    qi = pl.program_id(2)
    ki = pl.program_id(3)
    nk = pl.num_programs(3)
    q0 = qi * tq
    k0 = ki * tk

    @pl.when(ki == 0)
    def _init():
        q = q_ref[...] * (ATT_HD ** -0.5 * LOG2E)
        lane = lax.broadcasted_iota(jnp.int32, q.shape, 1)
        q_s[0:tq] = jnp.where(lane < ATT_HD, q, 0.0).astype(BF16)
        q_s[tq:2 * tq] = jnp.where(lane >= ATT_HD, q, 0.0).astype(BF16)
        m_s[...] = jnp.full(m_s.shape, NEG_INF, F32)
        l_s[...] = jnp.zeros(l_s.shape, F32)
        a_s[...] = jnp.zeros(a_s.shape, F32)

    def step(on_diagonal):
        kb = k_ref[...].astype(BF16)
        vb = jnp.concatenate([v_ref[...].astype(BF16), jnp.ones((tk, LANES), BF16)], axis=1)
        col = lax.broadcasted_iota(jnp.int32, (1, tk), 1)
        bias = (slope_ref[...] * LOG2E) * (k0 + col - q0).astype(F32)
        rg = 2 * tq // FLASH_ROW_GROUPS
        groups = [slice(g * rg, (g + 1) * rg) for g in range(FLASH_ROW_GROUPS)]
        scores = [_nt_dot(q_s[rows, :], kb) for rows in groups]
        for g, rows in enumerate(groups):
            s = scores[g] + bias
            if on_diagonal:
                r = lax.broadcasted_iota(jnp.int32, (rg, tk), 0) + (g * rg) % tq
                c = lax.broadcasted_iota(jnp.int32, (rg, tk), 1)
                s = jnp.where(c > r, NEG_INF, s)
            m_old = m_s[rows, :]
            m_new = jnp.maximum(m_old, jnp.max(s, axis=1, keepdims=True))
            alpha = jnp.exp2(m_old - m_new)
            p = jnp.exp2(s - jnp.tile(m_new, (1, tk // LANES)))
            pv = jnp.dot(p.astype(BF16), vb, preferred_element_type=F32)
            l_s[rows, :] = alpha * l_s[rows, :] + pv[:, ATT_W:]
            a_s[rows, :] = alpha * a_s[rows, :] + pv[:, :ATT_W]
            m_s[rows, :] = m_new

    pl.when(ki < qi)(functools.partial(step, False))
    pl.when(ki == qi)(functools.partial(step, True))

    @pl.when(ki == nk - 1)
    def _finish():
        lam = _diff_lambda(wl_ref[...], lam_init)
        w = a_s[...] / l_s[...]
        o = w[0:tq] - lam * w[tq:2 * tq]
        ms = jnp.mean(o * o, axis=1, keepdims=True)
        o_ref[...] = o * lax.rsqrt(ms + SUBLN_EPS) * g_ref[...] * (1.0 - lam_init)


def flash_prompt(qkv, slopes, w_lam, subln_g, *, batch, seq, tq, tk, lam_init):
    assert tq == tk
    nq = seq // tq
    nkb = seq // tk

    def kv_row(b, qi, ki):
        return b * nkb + jnp.minimum(ki, qi)

    kern = functools.partial(_flash_kernel, tq=tq, tk=tk, lam_init=lam_init)
    return pl.pallas_call(
        kern,
        grid=(batch, ATT_HEADS, nq, nkb),
        in_specs=[
            pl.BlockSpec((None, 1, tk), lambda b, h, qi, ki: (h, 0, 0)),
            pl.BlockSpec((4, ATT_HD), lambda b, h, qi, ki: (0, 0)),
            pl.BlockSpec((1, ATT_W), lambda b, h, qi, ki: (0, 0)),
            pl.BlockSpec((None, tq, ATT_W), lambda b, h, qi, ki: (0, b * nq + qi, h)),
            pl.BlockSpec((None, tk, ATT_W), lambda b, h, qi, ki: (1, kv_row(b, qi, ki), h)),
            pl.BlockSpec((None, tk, ATT_W), lambda b, h, qi, ki: (2, kv_row(b, qi, ki), h)),
        ],
        out_specs=pl.BlockSpec((tq, ATT_W), lambda b, h, qi, ki: (b * nq + qi, h)),
        out_shape=jax.ShapeDtypeStruct((batch * seq, ATT_HEADS * ATT_W), F32),
        scratch_shapes=[
            pltpu.VMEM((2 * tq, ATT_W), BF16), pltpu.VMEM((2 * tq, LANES), F32),
            pltpu.VMEM((2 * tq, LANES), F32), pltpu.VMEM((2 * tq, ATT_W), F32),
        ],
        compiler_params=_params("parallel", "parallel", "parallel", "arbitrary"),
        name="flash_prompt",
    )(slopes, w_lam, subln_g, qkv, qkv, qkv)


DECODE_PAGES_PER_STEP = 4


def _decode_kernel(pt_ref, q_ref, kn_ref, vn_ref, *rest, n_grp, page, past, lam_init):
    del pt_ref
    k_refs = rest[:n_grp]
    v_refs = rest[n_grp:2 * n_grp]
    bsel_ref, slope_ref, alibi_ref, g_ref, wl_ref, o_ref, m_s, l_s, aa_s, ab_s = rest[2 * n_grp:]
    pg = pl.program_id(1)
    n_steps = pl.num_programs(1)
    nh = ATT_HEADS
    q8 = q_ref[...] * (ATT_HD ** -0.5 * LOG2E)

    def half_sums(prod):
        return jnp.dot(prod.astype(BF16), bsel_ref[...], preferred_element_type=F32)

    def swap_halves(x):
        return pltpu.roll(x, ATT_HD, x.ndim - 1)

    @pl.when(pg == 0)
    def _init():
        m_s[...] = jnp.full(m_s.shape, NEG_INF, F32)
        for ref in (l_s, aa_s, ab_s):
            ref[...] = jnp.zeros(ref.shape, F32)

    slope = slope_ref[...]
    logits, shifts = [], []
    m_new = m_s[...]
    for g in range(n_grp):
        prod = (k_refs[g][...] * q8[None]).reshape(page * nh, ATT_W)
        s3 = half_sums(prod).reshape(page, nh, ATT_W) + alibi_ref[...]
        shift = slope * (past - (pg * n_grp + g) * page).astype(F32)
        m_new = jnp.maximum(m_new, jnp.max(s3, axis=0) - shift)
        logits.append(s3)
        shifts.append(shift)
    alpha = jnp.exp2(m_s[...] - m_new)
    l = alpha * l_s[...]
    acc_a = alpha * aa_s[...]
    acc_b = swap_halves(alpha) * ab_s[...]
    for g in range(n_grp):
        pe = jnp.exp2(logits[g] - (m_new + shifts[g])[None])
        v3 = v_refs[g][...]
        l = l + jnp.sum(pe, axis=0)
        acc_a = acc_a + jnp.sum(pe * v3, axis=0)
        acc_b = acc_b + jnp.sum(swap_halves(pe) * v3, axis=0)
    m_s[...] = m_new
    l_s[...] = l
    aa_s[...] = acc_a
    ab_s[...] = acc_b

    @pl.when(pg == n_steps - 1)
    def _finish():
        s_self = half_sums(q8 * kn_ref[...])
        m_n = jnp.maximum(m_new, s_self)
        al = jnp.exp2(m_new - m_n)
        p_self = jnp.exp2(s_self - m_n)
        lf = al * l + p_self
        vn = vn_ref[...]
        fa = al * acc_a + p_self * vn
        fb = swap_halves(al) * acc_b + swap_halves(p_self) * vn
        first = lax.broadcasted_iota(jnp.int32, fa.shape, 1) < ATT_HD
        lf_sw = swap_halves(lf)
        o1 = jnp.where(first, fa, fb) / jnp.where(first, lf, lf_sw)
        o2 = jnp.where(first, fb, fa) / jnp.where(first, lf_sw, lf)
        lam = _diff_lambda(wl_ref[...], lam_init)
        d = o1 - lam * o2
        ms = jnp.mean(d * d, axis=1, keepdims=True)
        o_ref[...] = d * lax.rsqrt(ms + SUBLN_EPS) * g_ref[...] * (1.0 - lam_init)


def decode_sample(page_table, q, k_new, v_new, cache_k, cache_v, w_lam, subln_g, *, layer, lam_init):
    bs, n_pages = page_table.shape
    page = cache_k.shape[2]
    nh = ATT_HEADS
    past = n_pages * page
    n_grp = DECODE_PAGES_PER_STEP
    assert n_pages % n_grp == 0
    half = jnp.arange(ATT_W)[:, None] // ATT_HD == jnp.arange(ATT_W)[None, :] // ATT_HD
    bsel = half.astype(BF16)
    slopes = jnp.exp2(-8.0 * jnp.arange(1, nh + 1, dtype=F32) / nh) * LOG2E
    slope = jnp.broadcast_to(slopes[:, None], (nh, ATT_W))
    alibi = jnp.arange(page, dtype=F32)[:, None, None] * slope[None]

    row_spec = pl.BlockSpec((None, nh, ATT_W), lambda b, p, pt: (b, 0, 0))
    const = lambda shape: pl.BlockSpec(shape, lambda b, p, pt: (0,) * len(shape))

    def page_spec(g):
        return pl.BlockSpec((None, None, page, nh, ATT_W),
                            lambda b, p, pt: (layer, pt[b, p * n_grp + g], 0, 0, 0))

    pages = [page_spec(g) for g in range(n_grp)]
    kern = functools.partial(_decode_kernel, n_grp=n_grp, page=page, past=past, lam_init=lam_init)
    return pl.pallas_call(
        kern,
        grid_spec=pltpu.PrefetchScalarGridSpec(
            num_scalar_prefetch=1,
            grid=(bs, n_pages // n_grp),
            in_specs=[row_spec, row_spec, row_spec] + pages + pages + [
                const((ATT_W, ATT_W)), const((nh, ATT_W)), const((page, nh, ATT_W)),
                const((1, ATT_W)), const((4, ATT_HD))],
            out_specs=pl.BlockSpec((None, nh, ATT_W), lambda b, p, pt: (b, 0, 0)),
            scratch_shapes=[pltpu.VMEM((nh, ATT_W), F32)] * 4,
        ),
        out_shape=jax.ShapeDtypeStruct((bs, nh, ATT_W), F32),
        compiler_params=_params("parallel", "arbitrary"),
        name="decode_sample",
    )(page_table, q, k_new, v_new, *([cache_k] * n_grp), *([cache_v] * n_grp),
      bsel, slope, alibi, subln_g[None, :], w_lam)


def _mm_res_ln_kernel(x_ref, w_ref, res_ref, g_ref, b_ref, o_ref, *, alpha):
    y = jnp.dot(x_ref[...].astype(BF16), w_ref[...], preferred_element_type=F32)
    o_ref[...] = _layer_norm(alpha * res_ref[...] + y, g_ref[...], b_ref[...])


def mm_res_ln(x, w, res, g, b, *, alpha, tm):
    m, k = x.shape
    n = w.shape[1]
    rows = lambda width: pl.BlockSpec((tm, width), lambda i: (i, 0))
    full = lambda shape: pl.BlockSpec(shape, lambda i: (0, 0))
    return pl.pallas_call(
        functools.partial(_mm_res_ln_kernel, alpha=alpha),
        grid=(m // tm,),
        in_specs=[rows(k), full((k, n)), rows(n), full((1, n)), full((1, n))],
        out_specs=rows(n),
        out_shape=jax.ShapeDtypeStruct((m, n), F32),
        compiler_params=_params("parallel"),
        name="mm_res_ln",
    )(x, w, res, g[None, :], b[None, :])


CARRY_ROWS = 8


def _conv_kernel(*refs, alpha, tm, chained):
    if chained:
        x_ref, win_ref, cw_ref, wout_ref, g_ref, b_ref, o_ref, u_ref, carry_s = refs
    else:
        x_ref, l0_ref, l1_ref, win_ref, cw_ref, wout_ref, g_ref, b_ref, o_ref, u_ref = refs
    d = x_ref.shape[1]
    x = x_ref[...]
    bch = jnp.dot(x.astype(BF16), win_ref[...], preferred_element_type=F32)
    b_g = bch[:, 0:d]
    u = bch[:, d:2 * d] * bch[:, 2 * d:3 * d]
    cw = cw_ref[...]
    if chained:
        i = pl.program_id(1)

        @pl.when(i == 0)
        def _zero_left():
            carry_s[...] = jnp.zeros(carry_s.shape, F32)

        prev = carry_s[...]
        row = lax.broadcasted_iota(jnp.int32, u.shape, 0)
        last = prev[CARRY_ROWS - 1:CARRY_ROWS]
        u1 = jnp.where(row == 0, last, pltpu.roll(u, 1, 0))
        u2 = jnp.where(row == 0, prev[CARRY_ROWS - 2:CARRY_ROWS - 1],
                       jnp.where(row == 1, last, pltpu.roll(u, 2, 0)))
        tail = u[tm - CARRY_ROWS:tm]
        carry_s[...] = tail
        u_ref[...] = tail
    else:
        u2 = l0_ref[...]
        u1 = l1_ref[...]
        u_ref[...] = u
    z = cw[0:1] * u2 + cw[1:2] * u1 + cw[2:3] * u
    y = jnp.dot((b_g * z).astype(BF16), wout_ref[...], preferred_element_type=F32)
    o_ref[...] = _layer_norm(alpha * x + y, g_ref[...], b_ref[...])


def conv_prompt(x, w_in, conv_w, w_out, g, b, *, batch, seq, alpha, tm):
    d = x.shape[1]
    nt = seq // tm
    rows = pl.BlockSpec((tm, d), lambda bi, i: (bi * nt + i, 0))
    full = lambda shape: pl.BlockSpec(shape, lambda bi, i: (0, 0))
    return pl.pallas_call(
        functools.partial(_conv_kernel, alpha=alpha, tm=tm, chained=True),
        grid=(batch, nt),
        in_specs=[rows, full((d, 3 * d)), full((CONV_W, d)), full((d, d)), full((1, d)), full((1, d))],
        out_specs=[rows, pl.BlockSpec((None, CARRY_ROWS, d), lambda bi, i: (bi, 0, 0))],
        out_shape=[jax.ShapeDtypeStruct((batch * seq, d), F32),
                   jax.ShapeDtypeStruct((batch, CARRY_ROWS, d), F32)],
        scratch_shapes=[pltpu.VMEM((CARRY_ROWS, d), F32)],
        compiler_params=_params("parallel", "arbitrary"),
        name="conv_prompt",
    )(x, w_in, conv_w, w_out, g[None, :], b[None, :])


def conv_sample(x, left0, left1, w_in, conv_w, w_out, g, b, *, alpha):
    m, d = x.shape
    full = lambda shape: pl.BlockSpec(shape, lambda i: (0, 0))
    return pl.pallas_call(
        functools.partial(_conv_kernel, alpha=alpha, tm=m, chained=False),
        grid=(1,),
        in_specs=[full((m, d)), full((m, d)), full((m, d)), full((d, 3 * d)), full((CONV_W, d)),
                  full((d, d)), full((1, d)), full((1, d))],
        out_specs=[full((m, d)), full((m, d))],
        out_shape=[jax.ShapeDtypeStruct((m, d), F32), jax.ShapeDtypeStruct((m, d), F32)],
        compiler_params=_params("arbitrary"),
        name="conv_sample",
    )(x, left0, left1, w_in, conv_w, w_out, g[None, :], b[None, :])


PACK = 4 // jnp.dtype(BF16).itemsize


def _pack_rows(x):
    return pltpu.bitcast(x.astype(BF16), jnp.uint32)


def _unpack_rows(x):
    return pltpu.bitcast(x, BF16)


def _replicate_word(x):
    if PACK == 1:
        return pltpu.bitcast(x, jnp.uint32)
    hi = pltpu.bitcast(x.astype(BF16).astype(F32), jnp.uint32)
    return hi | (hi >> 16)


def _top_values(x, count, store, want_rank=False):
    rank = jnp.full(x.shape, float(count), F32) if want_rank else None
    for r in range(count):
        mx = jnp.max(x, axis=0, keepdims=True)
        store(r, mx)
        hit = x == mx
        if want_rank:
            rank = jnp.where(hit, float(r), rank)
        if r + 1 < count:
            x = jnp.where(hit, -jnp.inf, x)
    return rank


def _select_kernel(q_ref, keys_ref, cnt_ref, e1_ref, r2_ref, e2_ref, s1_s, sv_s):
    k = PEER_TOPK
    dh = keys_ref.shape[2]
    per_slab = q_ref.shape[2] // dh
    for hc in range(2 * PEER_HEADS):
        h, second = divmod(hc, 2)
        qb = q_ref[hc // per_slab, :, (hc % per_slab) * dh:(hc % per_slab + 1) * dh].astype(BF16)
        s = _nt_dot(keys_ref[hc], qb)

        def store(r, mx, hc=hc):
            sv_s[hc, r:r + 1, :] = mx

        rank = _top_values(s, k, store, want_rank=bool(second))
        if second:
            r2_ref[0, h] = _pack_rows(rank)
            e2_ref[0, h] = _pack_rows(jnp.exp(s - sv_s[hc, 0:1, :]))
        else:
            s1_s[h] = s

    for h in range(PEER_HEADS):
        sv1 = sv_s[2 * h]
        sv2 = sv_s[2 * h + 1]
        sub = lax.broadcasted_iota(jnp.int32, (SUBLANES, LANES), 0)
        pieces = [sv1[0:1] + sv2]
        for a in range(2, SUBLANES + 1):
            sums = sv1[a - 1:a] + sv2[0:SUBLANES]
            pieces.append(sums if k // a >= SUBLANES else jnp.where(sub < k // a, sums, -jnp.inf))
        pieces.append(sv1[SUBLANES:k] + sv2[0:1])
        cand = jnp.concatenate(pieces, axis=0)
        tau_box = []
        _top_values(cand, k, lambda r, mx: tau_box.append(mx))
        tau = tau_box[-1]
        top = sv1[0:1] + sv2[0:1]
        z = jnp.sum(jnp.where(cand >= tau, jnp.exp(cand - top), 0.0), axis=0, keepdims=True)
        s1 = s1_s[h]
        cnt = jnp.zeros(s1.shape, F32)
        for b in range(k // 2):
            cnt = cnt + jnp.where(s1 + sv2[b:b + 1] >= tau, 1.0, 0.0)
        cnt_best = jnp.zeros((1, LANES), F32)
        for b in range(k // 2, k):
            cnt_best = cnt_best + jnp.where(sv1[0:1] + sv2[b:b + 1] >= tau, 1.0, 0.0)
        cnt_ref[0, h] = _replicate_word(jnp.where(s1 == sv1[0:1], cnt + cnt_best, cnt))
        e1_ref[0, h] = _replicate_word(jnp.exp(s1 - sv1[0:1]) * (1.0 / z))


def peer_select(q_slabs, keys):
    n_slab, t, slab_w = q_slabs.shape
    nhc, _, dh = keys.shape
    nchunk = t // LANES
    def out(rows, dtype):
        spec = pl.BlockSpec((1, PEER_HEADS, rows, LANES), lambda i: (i, 0, 0, 0))
        return spec, jax.ShapeDtypeStruct((nchunk, PEER_HEADS, rows, LANES), dtype)

    outs = [out(N_KEYS, jnp.uint32), out(N_KEYS, jnp.uint32),
            out(N_KEYS // PACK, jnp.uint32), out(N_KEYS // PACK, jnp.uint32)]
    return pl.pallas_call(
        _select_kernel,
        grid=(nchunk,),
        in_specs=[pl.BlockSpec((n_slab, LANES, slab_w), lambda i: (0, i, 0)),
                  pl.BlockSpec((nhc, N_KEYS, dh), lambda i: (0, 0, 0))],
        out_specs=[spec for spec, _ in outs],
        out_shape=[sds for _, sds in outs],
        scratch_shapes=[pltpu.VMEM((PEER_HEADS, N_KEYS, LANES), F32),
                        pltpu.VMEM((nhc, PEER_TOPK, LANES), F32)],
        compiler_params=_params("parallel"),
        name="peer_select",
    )(q_slabs, keys)


GATE_ROWS = 16
MXU_PIECES = 4
REGIONS_PER_HALF = 1


def _gelu(a):
    return 0.5 * a * (1.0 + lax.erf(a * (2.0 ** -0.5)))


def _peer_dense_kernel(xt_ref, u_ref, vt_ref, cnt_ref, e1_ref, r2_ref, e2_ref, res_ref,
                       g_ref, b_ref, o_ref, a0_s, a1_s, w0_s, w1_s, acc_s, *, alpha, te, tt, n_tiles):
    gstep = pl.program_id(1)
    n_steps = pl.num_programs(1)
    n_i = te // N_KEYS

    @pl.when(gstep == 0)
    def _init():
        for ref in (a0_s, a1_s, w0_s, w1_s, acc_s):
            ref[...] = jnp.zeros(ref.shape, ref.dtype)

    assert n_i == 4 and tt % (2 * LANES) == 0
    th = tt // 2

    def gate_block(iis, tc, rbs, i0, a_ref, w_ref):
        lanes = slice(tc * LANES, (tc + 1) * LANES)

        def row(ref, ii, h):
            r = ref[tc, h, pl.ds(i0 + ii, 1), :]
            return _unpack_rows(jnp.broadcast_to(r, (GATE_ROWS // PACK, LANES)))

        cnt = {(ii, h): row(cnt_ref, ii, h) for ii in iis for h in range(PEER_HEADS)}
        e1 = {(ii, h): row(e1_ref, ii, h) for ii in iis for h in range(PEER_HEADS)}
        zero = jnp.zeros((GATE_ROWS, LANES), BF16)
        for rb in rbs:
            keys = slice(rb * GATE_ROWS // PACK, (rb + 1) * GATE_ROWS // PACK)
            gate = {ii: zero for ii in iis}
            for h in range(PEER_HEADS):
                r2 = _unpack_rows(r2_ref[tc, h, keys, :])
                e2 = _unpack_rows(e2_ref[tc, h, keys, :])
                for ii in iis:
                    gate[ii] = gate[ii] + jnp.where(r2 < cnt[ii, h], e2, zero) * e1[ii, h]
            for ii in iis:
                out_rows = slice(ii * N_KEYS + rb * GATE_ROWS, ii * N_KEYS + (rb + 1) * GATE_ROWS)
                w_ref[out_rows, lanes] = gate[ii] * _gelu(a_ref[out_rows, lanes].astype(BF16))

    def half_step(tile, a_src, w_dst, w_src, vt0, a_dst, u0, region_base):
        i0 = jnp.clip(tile, 0, n_tiles - 1) * n_i
        n_tc = tt // LANES
        def quarter(r):
            iis = (2 * (r // 2), 2 * (r // 2) + 1)
            n_rb = N_KEYS // GATE_ROWS
            units = [(tc, range(part * n_rb // 2, (part + 1) * n_rb // 2))
                     for tc in range((r % 2) * n_tc // 2, (r % 2 + 1) * n_tc // 2) for part in range(2)]
            for piece in range(MXU_PIECES):
                if r < 2:
                    tok = slice(r * th, (r + 1) * th)
                    rows = slice(piece * (d // MXU_PIECES), (piece + 1) * (d // MXU_PIECES))
                    vt = _unpack_rows(vt_ref[rows.start // PACK:rows.stop // PACK, vt0:vt0 + te])
                    acc_s[rows, tok] += jnp.dot(vt, w_src[:, tok], preferred_element_type=F32)
                else:
                    tok = slice((r - 2) * th, (r - 1) * th)
                    rows = slice(piece * (te // MXU_PIECES), (piece + 1) * (te // MXU_PIECES))
                    u = _unpack_rows(u_ref[(u0 + rows.start) // PACK:(u0 + rows.stop) // PACK, :])
                    a_dst[rows, tok] = jnp.dot(u, xt_ref[:, tok], preferred_element_type=F32)
                for tc, rbs in units[piece * len(units) // MXU_PIECES:(piece + 1) * len(units) // MXU_PIECES]:
                    gate_block(iis, tc, rbs, i0, a_src, w_dst)

        per_region = n_i // REGIONS_PER_HALF
        for region in range(REGIONS_PER_HALF):
            @pl.when(gstep < n_steps + region_base + region)
            def _region(region=region):
                for r in range(region * per_region, (region + 1) * per_region):
                    quarter(r)

    d = acc_s.shape[0]
    half_step(2 * gstep - 1, a1_s, w1_s, w0_s, 0, a0_s, 0, 0)
    half_step(2 * gstep, a0_s, w0_s, w1_s, te, a1_s, te, REGIONS_PER_HALF)

    @pl.when(gstep == n_steps - 1)
    def _finish():
        y = acc_s[...].T
        o_ref[...] = _layer_norm(alpha * res_ref[...] + y, g_ref[...], b_ref[...])


def pack_rows_host(x):
    if PACK == 1:
        return lax.bitcast_convert_type(x, jnp.uint32)
    bits = [lax.bitcast_convert_type(x[r::PACK].astype(BF16), jnp.uint16).astype(jnp.uint32)
            for r in range(PACK)]
    return bits[0] | (bits[1] << 16)


def peer_dense(xt, u, vt, sel, res, g, b, *, alpha, tt, te):
    d, t = xt.shape
    n_exp = u.shape[0] * PACK
    n_tiles = n_exp // te
    assert n_tiles % 2 == 0
    n_pairs = n_tiles // 2
    nchunk = tt // LANES
    sel_specs = [pl.BlockSpec((nchunk,) + a.shape[1:], lambda ti, s: (ti, 0, 0, 0)) for a in sel]
    full = lambda shape: pl.BlockSpec(shape, lambda ti, s: (0, 0))
    u_spec = pl.BlockSpec((2 * te // PACK, d), lambda ti, s: (jnp.minimum(s, n_pairs - 1), 0))
    vt_spec = pl.BlockSpec((d // PACK, 2 * te), lambda ti, s: (0, jnp.maximum(s - 1, 0)))
    return pl.pallas_call(
        functools.partial(_peer_dense_kernel, alpha=alpha, te=te, tt=tt, n_tiles=n_tiles),
        grid=(t // tt, n_pairs + 1),
        in_specs=[pl.BlockSpec((d, tt), lambda ti, s: (0, ti)), u_spec, vt_spec,
                  *sel_specs,
                  pl.BlockSpec((tt, d), lambda ti, s: (ti, 0)),
                  full((1, d)), full((1, d))],
        out_specs=pl.BlockSpec((tt, d), lambda ti, s: (ti, 0)),
        out_shape=jax.ShapeDtypeStruct((t, d), F32),
        scratch_shapes=[pltpu.VMEM((te, tt), F32), pltpu.VMEM((te, tt), F32),
                        pltpu.VMEM((te, tt), BF16), pltpu.VMEM((te, tt), BF16),
                        pltpu.VMEM((d, tt), F32)],
        compiler_params=_params("parallel", "arbitrary"),
        name="peer_dense",
    )(xt, u, vt, *sel, res, g[None, :], b[None, :])


def _ple_kernel(h_ref, p_ref, wg_ref, wp_ref, o_ref):
    h = h_ref[...]
    gate = jax.nn.sigmoid(jnp.dot(h.astype(BF16), wg_ref[...], preferred_element_type=F32))
    proj = jnp.dot(p_ref[...].astype(BF16), wp_ref[...], preferred_element_type=F32)
    o_ref[...] = h + gate * proj


def ple_add(h, p, wg, wp, *, tm):
    m, d = h.shape
    pd = p.shape[1]
    rows = lambda width: pl.BlockSpec((tm, width), lambda i: (i, 0))
    full = lambda shape: pl.BlockSpec(shape, lambda i: (0, 0))
    return pl.pallas_call(
        _ple_kernel,
        grid=(m // tm,),
        in_specs=[rows(d), rows(pd), full((d, d)), full((pd, d))],
        out_specs=rows(d),
        out_shape=jax.ShapeDtypeStruct((m, d), F32),
        compiler_params=_params("parallel"),
        name="ple_add",
    )(h, p, wg, wp)


TOKEN_TILE = 512
PEER_EXPERT_TILE = 512


def _largest_tile(n, candidates):
    for c in candidates:
        if n % c == 0:
            return c
    raise ValueError(f"no tile in {candidates} divides {n}")


def kernel(x_prompt, x_sample, cache_k, cache_v, state_conv, page_table, p_prompt, p_sample,
           ln_g, ln_b, w_attn_qkv, w_attn_lambda, attn_subln_g, w_attn_o,
           w_conv_in, conv_w, w_conv_out, w_peer_q, peer_keys, peer_u, peer_v,
           w_ple_gate, w_ple_proj):
    batch, seq, d = x_prompt.shape
    bs = x_sample.shape[0]
    assert x_sample.shape[1] == 1
    depth = ln_g.shape[0]
    tp = batch * seq
    t = tp + bs
    tm = TOKEN_TILE
    t_pad = -(-t // tm) * tm
    alpha = (2 * depth) ** 0.25
    width = ATT_HEADS * ATT_W
    tseq = _largest_tile(seq, (512, 256, 128))

    def joint(prompt_rows, sample_rows):
        pad = jnp.zeros((t_pad - t, prompt_rows.shape[1]), F32)
        return jnp.concatenate([prompt_rows, sample_rows, pad], axis=0)

    h = joint(x_prompt.reshape(tp, d), x_sample.reshape(bs, d))
    slopes = jnp.exp2(-8.0 * jnp.arange(1, ATT_HEADS + 1, dtype=F32) / ATT_HEADS)
    slopes_b = jnp.broadcast_to(slopes[:, None, None], (ATT_HEADS, 1, tseq))

    kp_l, vp_l, ks_l, vs_l, cp_l, cs_l = [], [], [], [], [], []
    for i in range(depth):
        j = i // N_MIXERS
        if i % N_MIXERS == 0:
            lam_init = 0.8 - 0.6 * math.exp(-0.3 * i)
            qkv = matmul_slabs(h, w_attn_qkv[j].astype(BF16), tm=tm, tn=width)
            g_sub = attn_subln_g[j]
            o_p = flash_prompt(qkv, slopes_b, w_attn_lambda[j], g_sub[None, :], batch=batch, seq=seq,
                               tq=tseq, tk=tseq, lam_init=lam_init)
            qkv_s = qkv[:, tp:t].reshape(3, bs, ATT_HEADS, ATT_W)
            o_s = decode_sample(page_table, qkv_s[0], qkv_s[1], qkv_s[2], cache_k, cache_v,
                                w_attn_lambda[j], g_sub, layer=j, lam_init=lam_init)
            o = joint(o_p, o_s.reshape(bs, width))
            h = mm_res_ln(o, w_attn_o[j].astype(BF16), h, ln_g[i, 0], ln_b[i, 0], alpha=alpha, tm=tm)
            kp_l.append(qkv[1, :tp].reshape(batch, seq, ATT_HEADS, ATT_W))
            vp_l.append(qkv[2, :tp].reshape(batch, seq, ATT_HEADS, ATT_W))
            ks_l.append(qkv_s[1].reshape(bs, 1, ATT_HEADS, ATT_W))
            vs_l.append(qkv_s[2].reshape(bs, 1, ATT_HEADS, ATT_W))
        else:
            w_in = w_conv_in[j].astype(BF16)
            w_out = w_conv_out[j].astype(BF16)
            hp, tail = conv_prompt(h[:tp], w_in, conv_w[j], w_out, ln_g[i, 0], ln_b[i, 0],
                                   batch=batch, seq=seq, alpha=alpha, tm=tseq)
            left = state_conv[j]
            hs, u_s = conv_sample(h[tp:t], left[:, 0], left[:, 1], w_in, conv_w[j], w_out,
                                  ln_g[i, 0], ln_b[i, 0], alpha=alpha)
            h = joint(hp, hs)
            cp_l.append(tail[:, CARRY_ROWS - (CONV_W - 1):])
            cs_l.append(jnp.stack([left[:, 1], u_s], axis=1))

        n_hc = 2 * PEER_HEADS
        dh = peer_keys.shape[-1]
        q_slabs = matmul_slabs(h, w_peer_q[i].astype(BF16), tm=tm, tn=d)
        keys = peer_keys[i].reshape(n_hc, N_KEYS, dh).astype(BF16)
        sel = peer_select(q_slabs, keys)
        h = peer_dense(h.T.astype(BF16), pack_rows_host(peer_u[i]), pack_rows_host(peer_v[i].T), sel, h,
                       ln_g[i, 1], ln_b[i, 1], alpha=alpha, tt=tm, te=PEER_EXPERT_TILE)

        p = joint(p_prompt[i].reshape(tp, -1), p_sample[i].reshape(bs, -1))
        h = ple_add(h, p, w_ple_gate[i].astype(BF16), w_ple_proj[i].astype(BF16), tm=tm)

    return (h[:tp].reshape(batch, seq, d), h[tp:t].reshape(bs, 1, d),
            jnp.stack(kp_l), jnp.stack(vp_l), jnp.stack(ks_l), jnp.stack(vs_l),
            jnp.stack(cp_l), jnp.stack(cs_l))
```

```python
import functools
import math

import jax
import jax.numpy as jnp
from jax import lax
from jax.experimental import pallas as pl
from jax.experimental.pallas import tpu as pltpu

BF16 = jnp.bfloat16
F32 = jnp.float32

ATT_HEADS = 8
ATT_HD = 64
ATT_W = 2 * ATT_HD
N_MIXERS = 2
CONV_W = 3
PEER_HEADS = 8
N_KEYS = 128
PEER_TOPK = 16
LN_EPS = 1e-5
SUBLN_EPS = 1e-5
NEG_INF = -1e30
LOG2E = math.log2(math.e)
LANES = 128
SUBLANES = 8
VMEM_LIMIT = 56 * 1024 * 1024


def _params(*sem, flags=None):
    return pltpu.CompilerParams(dimension_semantics=sem, vmem_limit_bytes=VMEM_LIMIT, flags=flags)


def _nt_dot(a, b):
    return lax.dot_general(a, b, (((1,), (1,)), ((), ())), preferred_element_type=F32)


def _layer_norm(z, g, b):
    mu = jnp.mean(z, axis=-1, keepdims=True)
    zc = z - mu
    var = jnp.mean(zc * zc, axis=-1, keepdims=True)
    return zc * lax.rsqrt(var + LN_EPS) * g + b


def _diff_lambda(wl, lam_init):
    a = jnp.sum(wl[0:1] * wl[1:2], axis=1, keepdims=True)
    b = jnp.sum(wl[2:3] * wl[3:4], axis=1, keepdims=True)
    return jnp.exp(a) - jnp.exp(b) + lam_init


def _mm_kernel(x_ref, w_ref, o_ref):
    o_ref[...] = jnp.dot(x_ref[...].astype(BF16), w_ref[...], preferred_element_type=F32)


def matmul_slabs(x, w, *, tm, tn):
    m, k = x.shape
    n = w.shape[1]
    return pl.pallas_call(
        _mm_kernel,
        grid=(m // tm, n // tn),
        in_specs=[pl.BlockSpec((tm, k), lambda i, j: (i, 0)),
                  pl.BlockSpec((k, tn), lambda i, j: (0, j))],
        out_specs=pl.BlockSpec((None, tm, tn), lambda i, j: (j, i, 0)),
        out_shape=jax.ShapeDtypeStruct((n // tn, m, tn), F32),
        compiler_params=_params("parallel", "parallel"),
        name="matmul_slabs",
    )(x, w)


FLASH_ROW_GROUPS = 4


def _flash_kernel(slope_ref, wl_ref, g_ref, q_ref, k_ref, v_ref, o_ref,
                  q_s, m_s, l_s, a_s, *, tq, tk, lam_init):
    qi = pl.program_id(2)
    ki = pl.program_id(3)
    nk = pl.num_programs(3)
    q0 = qi * tq
    k0 = ki * tk

    @pl.when(ki == 0)
    def _init():
        q = q_ref[...] * (ATT_HD ** -0.5 * LOG2E)
        lane = lax.broadcasted_iota(jnp.int32, q.shape, 1)
        q_s[0:tq] = jnp.where(lane < ATT_HD, q, 0.0).astype(BF16)
        q_s[tq:2 * tq] = jnp.where(lane >= ATT_HD, q, 0.0).astype(BF16)
        m_s[...] = jnp.full(m_s.shape, NEG_INF, F32)
        l_s[...] = jnp.zeros(l_s.shape, F32)
        a_s[...] = jnp.zeros(a_s.shape, F32)

    def step(on_diagonal):
        kb = k_ref[...].astype(BF16)
        vb = jnp.concatenate([v_ref[...].astype(BF16), jnp.ones((tk, LANES), BF16)], axis=1)
        col = lax.broadcasted_iota(jnp.int32, (1, tk), 1)
        bias = (slope_ref[...] * LOG2E) * (k0 + col - q0).astype(F32)
        rg = 2 * tq // FLASH_ROW_GROUPS
        groups = [slice(g * rg, (g + 1) * rg) for g in range(FLASH_ROW_GROUPS)]
        scores = [_nt_dot(q_s[rows, :], kb) for rows in groups]
        for g, rows in enumerate(groups):
            s = scores[g] + bias
            if on_diagonal:
                r = lax.broadcasted_iota(jnp.int32, (rg, tk), 0) + (g * rg) % tq
                c = lax.broadcasted_iota(jnp.int32, (rg, tk), 1)
                s = jnp.where(c > r, NEG_INF, s)
            m_old = m_s[rows, :]
            m_new = jnp.maximum(m_old, jnp.max(s, axis=1, keepdims=True))
            alpha = jnp.exp2(m_old - m_new)
            p = jnp.exp2(s - jnp.tile(m_new, (1, tk // LANES)))
            pv = jnp.dot(p.astype(BF16), vb, preferred_element_type=F32)
            l_s[rows, :] = alpha * l_s[rows, :] + pv[:, ATT_W:]
            a_s[rows, :] = alpha * a_s[rows, :] + pv[:, :ATT_W]
            m_s[rows, :] = m_new

    pl.when(ki < qi)(functools.partial(step, False))
    pl.when(ki == qi)(functools.partial(step, True))

    @pl.when(ki == nk - 1)
    def _finish():
        lam = _diff_lambda(wl_ref[...], lam_init)
        w = a_s[...] / l_s[...]
        o = w[0:tq] - lam * w[tq:2 * tq]
        ms = jnp.mean(o * o, axis=1, keepdims=True)
        o_ref[...] = o * lax.rsqrt(ms + SUBLN_EPS) * g_ref[...] * (1.0 - lam_init)


def flash_prompt(qkv, slopes, w_lam, subln_g, *, batch, seq, tq, tk, lam_init):
    assert tq == tk
    nq = seq // tq
    nkb = seq // tk

    def kv_row(b, qi, ki):
        return b * nkb + jnp.minimum(ki, qi)

    kern = functools.partial(_flash_kernel, tq=tq, tk=tk, lam_init=lam_init)
    return pl.pallas_call(
        kern,
        grid=(batch, ATT_HEADS, nq, nkb),
        in_specs=[
            pl.BlockSpec((None, 1, tk), lambda b, h, qi, ki: (h, 0, 0)),
            pl.BlockSpec((4, ATT_HD), lambda b, h, qi, ki: (0, 0)),
            pl.BlockSpec((1, ATT_W), lambda b, h, qi, ki: (0, 0)),
            pl.BlockSpec((None, tq, ATT_W), lambda b, h, qi, ki: (0, b * nq + qi, h)),
            pl.BlockSpec((None, tk, ATT_W), lambda b, h, qi, ki: (1, kv_row(b, qi, ki), h)),
            pl.BlockSpec((None, tk, ATT_W), lambda b, h, qi, ki: (2, kv_row(b, qi, ki), h)),
        ],
        out_specs=pl.BlockSpec((tq, ATT_W), lambda b, h, qi, ki: (b * nq + qi, h)),
        out_shape=jax.ShapeDtypeStruct((batch * seq, ATT_HEADS * ATT_W), F32),
        scratch_shapes=[
            pltpu.VMEM((2 * tq, ATT_W), BF16), pltpu.VMEM((2 * tq, LANES), F32),
            pltpu.VMEM((2 * tq, LANES), F32), pltpu.VMEM((2 * tq, ATT_W), F32),
        ],
        compiler_params=_params("parallel", "parallel", "parallel", "arbitrary"),
        name="flash_prompt",
    )(slopes, w_lam, subln_g, qkv, qkv, qkv)


DECODE_PAGES_PER_STEP = 4


def _decode_kernel(pt_ref, q_ref, kn_ref, vn_ref, *rest, n_grp, page, past, lam_init):
    del pt_ref
    k_refs = rest[:n_grp]
    v_refs = rest[n_grp:2 * n_grp]
    bsel_ref, slope_ref, alibi_ref, g_ref, wl_ref, o_ref, m_s, l_s, aa_s, ab_s = rest[2 * n_grp:]
    pg = pl.program_id(1)
    n_steps = pl.num_programs(1)
    nh = ATT_HEADS
    q8 = q_ref[...] * (ATT_HD ** -0.5 * LOG2E)

    def half_sums(prod):
        return jnp.dot(prod.astype(BF16), bsel_ref[...], preferred_element_type=F32)

    def swap_halves(x):
        return pltpu.roll(x, ATT_HD, x.ndim - 1)

    @pl.when(pg == 0)
    def _init():
        m_s[...] = jnp.full(m_s.shape, NEG_INF, F32)
        for ref in (l_s, aa_s, ab_s):
            ref[...] = jnp.zeros(ref.shape, F32)

    slope = slope_ref[...]
    logits, shifts = [], []
    m_new = m_s[...]
    for g in range(n_grp):
        prod = (k_refs[g][...] * q8[None]).reshape(page * nh, ATT_W)
        s3 = half_sums(prod).reshape(page, nh, ATT_W) + alibi_ref[...]
        shift = slope * (past - (pg * n_grp + g) * page).astype(F32)
        m_new = jnp.maximum(m_new, jnp.max(s3, axis=0) - shift)
        logits.append(s3)
        shifts.append(shift)
    alpha = jnp.exp2(m_s[...] - m_new)
    l = alpha * l_s[...]
    acc_a = alpha * aa_s[...]
    acc_b = swap_halves(alpha) * ab_s[...]
    for g in range(n_grp):
        pe = jnp.exp2(logits[g] - (m_new + shifts[g])[None])
        v3 = v_refs[g][...]
        l = l + jnp.sum(pe, axis=0)
        acc_a = acc_a + jnp.sum(pe * v3, axis=0)
        acc_b = acc_b + jnp.sum(swap_halves(pe) * v3, axis=0)
    m_s[...] = m_new
    l_s[...] = l
    aa_s[...] = acc_a
    ab_s[...] = acc_b

    @pl.when(pg == n_steps - 1)
    def _finish():
        s_self = half_sums(q8 * kn_ref[...])
        m_n = jnp.maximum(m_new, s_self)
        al = jnp.exp2(m_new - m_n)
        p_self = jnp.exp2(s_self - m_n)
        lf = al * l + p_self
        vn = vn_ref[...]
        fa = al * acc_a + p_self * vn
        fb = swap_halves(al) * acc_b + swap_halves(p_self) * vn
        first = lax.broadcasted_iota(jnp.int32, fa.shape, 1) < ATT_HD
        lf_sw = swap_halves(lf)
        o1 = jnp.where(first, fa, fb) / jnp.where(first, lf, lf_sw)
        o2 = jnp.where(first, fb, fa) / jnp.where(first, lf_sw, lf)
        lam = _diff_lambda(wl_ref[...], lam_init)
        d = o1 - lam * o2
        ms = jnp.mean(d * d, axis=1, keepdims=True)
        o_ref[...] = d * lax.rsqrt(ms + SUBLN_EPS) * g_ref[...] * (1.0 - lam_init)


def decode_sample(page_table, q, k_new, v_new, cache_k, cache_v, w_lam, subln_g, *, layer, lam_init):
    bs, n_pages = page_table.shape
    page = cache_k.shape[2]
    nh = ATT_HEADS
    past = n_pages * page
    n_grp = DECODE_PAGES_PER_STEP
    assert n_pages % n_grp == 0
    half = jnp.arange(ATT_W)[:, None] // ATT_HD == jnp.arange(ATT_W)[None, :] // ATT_HD
    bsel = half.astype(BF16)
    slopes = jnp.exp2(-8.0 * jnp.arange(1, nh + 1, dtype=F32) / nh) * LOG2E
    slope = jnp.broadcast_to(slopes[:, None], (nh, ATT_W))
    alibi = jnp.arange(page, dtype=F32)[:, None, None] * slope[None]

    row_spec = pl.BlockSpec((None, nh, ATT_W), lambda b, p, pt: (b, 0, 0))
    const = lambda shape: pl.BlockSpec(shape, lambda b, p, pt: (0,) * len(shape))

    def page_spec(g):
        return pl.BlockSpec((None, None, page, nh, ATT_W),
                            lambda b, p, pt: (layer, pt[b, p * n_grp + g], 0, 0, 0))

    pages = [page_spec(g) for g in range(n_grp)]
    kern = functools.partial(_decode_kernel, n_grp=n_grp, page=page, past=past, lam_init=lam_init)
    return pl.pallas_call(
        kern,
        grid_spec=pltpu.PrefetchScalarGridSpec(
            num_scalar_prefetch=1,
            grid=(bs, n_pages // n_grp),
            in_specs=[row_spec, row_spec, row_spec] + pages + pages + [
                const((ATT_W, ATT_W)), const((nh, ATT_W)), const((page, nh, ATT_W)),
                const((1, ATT_W)), const((4, ATT_HD))],
            out_specs=pl.BlockSpec((None, nh, ATT_W), lambda b, p, pt: (b, 0, 0)),
            scratch_shapes=[pltpu.VMEM((nh, ATT_W), F32)] * 4,
        ),
        out_shape=jax.ShapeDtypeStruct((bs, nh, ATT_W), F32),
        compiler_params=_params("parallel", "arbitrary"),
        name="decode_sample",
    )(page_table, q, k_new, v_new, *([cache_k] * n_grp), *([cache_v] * n_grp),
      bsel, slope, alibi, subln_g[None, :], w_lam)


def _mm_res_ln_kernel(x_ref, w_ref, res_ref, g_ref, b_ref, o_ref, *, alpha):
    y = jnp.dot(x_ref[...].astype(BF16), w_ref[...], preferred_element_type=F32)
    o_ref[...] = _layer_norm(alpha * res_ref[...] + y, g_ref[...], b_ref[...])


def mm_res_ln(x, w, res, g, b, *, alpha, tm):
    m, k = x.shape
    n = w.shape[1]
    rows = lambda width: pl.BlockSpec((tm, width), lambda i: (i, 0))
    full = lambda shape: pl.BlockSpec(shape, lambda i: (0, 0))
    return pl.pallas_call(
        functools.partial(_mm_res_ln_kernel, alpha=alpha),
        grid=(m // tm,),
        in_specs=[rows(k), full((k, n)), rows(n), full((1, n)), full((1, n))],
        out_specs=rows(n),
        out_shape=jax.ShapeDtypeStruct((m, n), F32),
        compiler_params=_params("parallel"),
        name="mm_res_ln",
    )(x, w, res, g[None, :], b[None, :])


CARRY_ROWS = 8


def _conv_kernel(*refs, alpha, tm, chained):
    if chained:
        x_ref, win_ref, cw_ref, wout_ref, g_ref, b_ref, o_ref, u_ref, carry_s = refs
    else:
        x_ref, l0_ref, l1_ref, win_ref, cw_ref, wout_ref, g_ref, b_ref, o_ref, u_ref = refs
    d = x_ref.shape[1]
    x = x_ref[...]
    bch = jnp.dot(x.astype(BF16), win_ref[...], preferred_element_type=F32)
    b_g = bch[:, 0:d]
    u = bch[:, d:2 * d] * bch[:, 2 * d:3 * d]
    cw = cw_ref[...]
    if chained:
        i = pl.program_id(1)

        @pl.when(i == 0)
        def _zero_left():
            carry_s[...] = jnp.zeros(carry_s.shape, F32)

        prev = carry_s[...]
        row = lax.broadcasted_iota(jnp.int32, u.shape, 0)
        last = prev[CARRY_ROWS - 1:CARRY_ROWS]
        u1 = jnp.where(row == 0, last, pltpu.roll(u, 1, 0))
        u2 = jnp.where(row == 0, prev[CARRY_ROWS - 2:CARRY_ROWS - 1],
                       jnp.where(row == 1, last, pltpu.roll(u, 2, 0)))
        tail = u[tm - CARRY_ROWS:tm]
        carry_s[...] = tail
        u_ref[...] = tail
    else:
        u2 = l0_ref[...]
        u1 = l1_ref[...]
        u_ref[...] = u
    z = cw[0:1] * u2 + cw[1:2] * u1 + cw[2:3] * u
    y = jnp.dot((b_g * z).astype(BF16), wout_ref[...], preferred_element_type=F32)
    o_ref[...] = _layer_norm(alpha * x + y, g_ref[...], b_ref[...])


def conv_prompt(x, w_in, conv_w, w_out, g, b, *, batch, seq, alpha, tm):
    d = x.shape[1]
    nt = seq // tm
    rows = pl.BlockSpec((tm, d), lambda bi, i: (bi * nt + i, 0))
    full = lambda shape: pl.BlockSpec(shape, lambda bi, i: (0, 0))
    return pl.pallas_call(
        functools.partial(_conv_kernel, alpha=alpha, tm=tm, chained=True),
        grid=(batch, nt),
        in_specs=[rows, full((d, 3 * d)), full((CONV_W, d)), full((d, d)), full((1, d)), full((1, d))],
        out_specs=[rows, pl.BlockSpec((None, CARRY_ROWS, d), lambda bi, i: (bi, 0, 0))],
        out_shape=[jax.ShapeDtypeStruct((batch * seq, d), F32),
                   jax.ShapeDtypeStruct((batch, CARRY_ROWS, d), F32)],
        scratch_shapes=[pltpu.VMEM((CARRY_ROWS, d), F32)],
        compiler_params=_params("parallel", "arbitrary"),
        name="conv_prompt",
    )(x, w_in, conv_w, w_out, g[None, :], b[None, :])


def conv_sample(x, left0, left1, w_in, conv_w, w_out, g, b, *, alpha):
    m, d = x.shape
    full = lambda shape: pl.BlockSpec(shape, lambda i: (0, 0))
    return pl.pallas_call(
        functools.partial(_conv_kernel, alpha=alpha, tm=m, chained=False),
        grid=(1,),
        in_specs=[full((m, d)), full((m, d)), full((m, d)), full((d, 3 * d)), full((CONV_W, d)),
                  full((d, d)), full((1, d)), full((1, d))],
        out_specs=[full((m, d)), full((m, d))],
        out_shape=[jax.ShapeDtypeStruct((m, d), F32), jax.ShapeDtypeStruct((m, d), F32)],
        compiler_params=_params("arbitrary"),
        name="conv_sample",
    )(x, left0, left1, w_in, conv_w, w_out, g[None, :], b[None, :])


PACK = 4 // jnp.dtype(BF16).itemsize


def _pack_rows(x):
    return pltpu.bitcast(x.astype(BF16), jnp.uint32)


def _unpack_rows(x):
    return pltpu.bitcast(x, BF16)


def _replicate_word(x):
    if PACK == 1:
        return pltpu.bitcast(x, jnp.uint32)
    hi = pltpu.bitcast(x.astype(BF16).astype(F32), jnp.uint32)
    return hi | (hi >> 16)


def _top_values(x, count, store, want_rank=False):
    rank = jnp.full(x.shape, float(count), F32) if want_rank else None
    for r in range(count):
        mx = jnp.max(x, axis=0, keepdims=True)
        store(r, mx)
        hit = x == mx
        if want_rank:
            rank = jnp.where(hit, float(r), rank)
        if r + 1 < count:
            x = jnp.where(hit, -jnp.inf, x)
    return rank


def _select_kernel(q_ref, keys_ref, cnt_ref, e1_ref, r2_ref, e2_ref, s1_s, sv_s):
    k = PEER_TOPK
    dh = keys_ref.shape[2]
    per_slab = q_ref.shape[2] // dh
    for hc in range(2 * PEER_HEADS):
        h, second = divmod(hc, 2)
        qb = q_ref[hc // per_slab, :, (hc % per_slab) * dh:(hc % per_slab + 1) * dh].astype(BF16)
        s = _nt_dot(keys_ref[hc], qb)

        def store(r, mx, hc=hc):
            sv_s[hc, r:r + 1, :] = mx

        rank = _top_values(s, k, store, want_rank=bool(second))
        if second:
            r2_ref[0, h] = _pack_rows(rank)
            e2_ref[0, h] = _pack_rows(jnp.exp(s - sv_s[hc, 0:1, :]))
        else:
            s1_s[h] = s

    for h in range(PEER_HEADS):
        sv1 = sv_s[2 * h]
        sv2 = sv_s[2 * h + 1]
        sub = lax.broadcasted_iota(jnp.int32, (SUBLANES, LANES), 0)
        pieces = [sv1[0:1] + sv2]
        for a in range(2, SUBLANES + 1):
            sums = sv1[a - 1:a] + sv2[0:SUBLANES]
            pieces.append(sums if k // a >= SUBLANES else jnp.where(sub < k // a, sums, -jnp.inf))
        pieces.append(sv1[SUBLANES:k] + sv2[0:1])
        cand = jnp.concatenate(pieces, axis=0)
        tau_box = []
        _top_values(cand, k, lambda r, mx: tau_box.append(mx))
        tau = tau_box[-1]
        top = sv1[0:1] + sv2[0:1]
        z = jnp.sum(jnp.where(cand >= tau, jnp.exp(cand - top), 0.0), axis=0, keepdims=True)
        s1 = s1_s[h]
        cnt = jnp.zeros(s1.shape, F32)
        for b in range(k // 2):
            cnt = cnt + jnp.where(s1 + sv2[b:b + 1] >= tau, 1.0, 0.0)
        cnt_best = jnp.zeros((1, LANES), F32)
        for b in range(k // 2, k):
            cnt_best = cnt_best + jnp.where(sv1[0:1] + sv2[b:b + 1] >= tau, 1.0, 0.0)
        cnt_ref[0, h] = _replicate_word(jnp.where(s1 == sv1[0:1], cnt + cnt_best, cnt))
        e1_ref[0, h] = _replicate_word(jnp.exp(s1 - sv1[0:1]) * (1.0 / z))


def peer_select(q_slabs, keys):
    n_slab, t, slab_w = q_slabs.shape
    nhc, _, dh = keys.shape
    nchunk = t // LANES
    def out(rows, dtype):
        spec = pl.BlockSpec((1, PEER_HEADS, rows, LANES), lambda i: (i, 0, 0, 0))
        return spec, jax.ShapeDtypeStruct((nchunk, PEER_HEADS, rows, LANES), dtype)

    outs = [out(N_KEYS, jnp.uint32), out(N_KEYS, jnp.uint32),
            out(N_KEYS // PACK, jnp.uint32), out(N_KEYS // PACK, jnp.uint32)]
    return pl.pallas_call(
        _select_kernel,
        grid=(nchunk,),
        in_specs=[pl.BlockSpec((n_slab, LANES, slab_w), lambda i: (0, i, 0)),
                  pl.BlockSpec((nhc, N_KEYS, dh), lambda i: (0, 0, 0))],
        out_specs=[spec for spec, _ in outs],
        out_shape=[sds for _, sds in outs],
        scratch_shapes=[pltpu.VMEM((PEER_HEADS, N_KEYS, LANES), F32),
                        pltpu.VMEM((nhc, PEER_TOPK, LANES), F32)],
        compiler_params=_params("parallel"),
        name="peer_select",
    )(q_slabs, keys)


GATE_ROWS = 16
MXU_PIECES = 4
REGIONS_PER_HALF = 1


def _gelu(a):
    return 0.5 * a * (1.0 + lax.erf(a * (2.0 ** -0.5)))


def _peer_dense_kernel(xt_ref, u_ref, vt_ref, cnt_ref, e1_ref, r2_ref, e2_ref, res_ref,
                       g_ref, b_ref, o_ref, a0_s, a1_s, w0_s, w1_s, acc_s, *, alpha, te, tt, n_tiles):
    gstep = pl.program_id(1)
    n_steps = pl.num_programs(1)
    n_i = te // N_KEYS

    @pl.when(gstep == 0)
    def _init():
        for ref in (a0_s, a1_s, w0_s, w1_s, acc_s):
            ref[...] = jnp.zeros(ref.shape, ref.dtype)

    assert n_i == 4 and tt % (2 * LANES) == 0
    th = tt // 2

    def gate_block(iis, tc, rbs, i0, a_ref, w_ref):
        lanes = slice(tc * LANES, (tc + 1) * LANES)

        def row(ref, ii, h):
            r = ref[tc, h, pl.ds(i0 + ii, 1), :]
            return _unpack_rows(jnp.broadcast_to(r, (GATE_ROWS // PACK, LANES)))

        cnt = {(ii, h): row(cnt_ref, ii, h) for ii in iis for h in range(PEER_HEADS)}
        e1 = {(ii, h): row(e1_ref, ii, h) for ii in iis for h in range(PEER_HEADS)}
        zero = jnp.zeros((GATE_ROWS, LANES), BF16)
        for rb in rbs:
            keys = slice(rb * GATE_ROWS // PACK, (rb + 1) * GATE_ROWS // PACK)
            gate = {ii: zero for ii in iis}
            for h in range(PEER_HEADS):
                r2 = _unpack_rows(r2_ref[tc, h, keys, :])
                e2 = _unpack_rows(e2_ref[tc, h, keys, :])
                for ii in iis:
                    gate[ii] = gate[ii] + jnp.where(r2 < cnt[ii, h], e2, zero) * e1[ii, h]
            for ii in iis:
                out_rows = slice(ii * N_KEYS + rb * GATE_ROWS, ii * N_KEYS + (rb + 1) * GATE_ROWS)
                w_ref[out_rows, lanes] = gate[ii] * _gelu(a_ref[out_rows, lanes].astype(BF16))

    def half_step(tile, a_src, w_dst, w_src, vt0, a_dst, u0, region_base):
        i0 = jnp.clip(tile, 0, n_tiles - 1) * n_i
        n_tc = tt // LANES
        def quarter(r):
            iis = (2 * (r // 2), 2 * (r // 2) + 1)
            n_rb = N_KEYS // GATE_ROWS
            units = [(tc, range(part * n_rb // 2, (part + 1) * n_rb // 2))
                     for tc in range((r % 2) * n_tc // 2, (r % 2 + 1) * n_tc // 2) for part in range(2)]
            for piece in range(MXU_PIECES):
                if r < 2:
                    tok = slice(r * th, (r + 1) * th)
                    rows = slice(piece * (d // MXU_PIECES), (piece + 1) * (d // MXU_PIECES))
                    vt = _unpack_rows(vt_ref[rows.start // PACK:rows.stop // PACK, vt0:vt0 + te])
                    acc_s[rows, tok] += jnp.dot(vt, w_src[:, tok], preferred_element_type=F32)
                else:
                    tok = slice((r - 2) * th, (r - 1) * th)
                    rows = slice(piece * (te // MXU_PIECES), (piece + 1) * (te // MXU_PIECES))
                    u = _unpack_rows(u_ref[(u0 + rows.start) // PACK:(u0 + rows.stop) // PACK, :])
                    a_dst[rows, tok] = jnp.dot(u, xt_ref[:, tok], preferred_element_type=F32)
                for tc, rbs in units[piece * len(units) // MXU_PIECES:(piece + 1) * len(units) // MXU_PIECES]:
                    gate_block(iis, tc, rbs, i0, a_src, w_dst)

        per_region = n_i // REGIONS_PER_HALF
        for region in range(REGIONS_PER_HALF):
            @pl.when(gstep < n_steps + region_base + region)
            def _region(region=region):
                for r in range(region * per_region, (region + 1) * per_region):
                    quarter(r)

    d = acc_s.shape[0]
    half_step(2 * gstep - 1, a1_s, w1_s, w0_s, 0, a0_s, 0, 0)
    half_step(2 * gstep, a0_s, w0_s, w1_s, te, a1_s, te, REGIONS_PER_HALF)

    @pl.when(gstep == n_steps - 1)
    def _finish():
        y = acc_s[...].T
        o_ref[...] = _layer_norm(alpha * res_ref[...] + y, g_ref[...], b_ref[...])


def _pack_table_kernel(x_ref, o_ref, *, transpose):
    x = x_ref[...]
    o_ref[...] = _pack_rows(x.T if transpose else x)


def pack_table(x, *, transpose, tile=1024):
    rows, cols = x.shape
    if transpose:
        out_shape = (cols // PACK, rows)
        out_spec = pl.BlockSpec((cols // PACK, tile), lambda i: (0, i))
    else:
        out_shape = (rows // PACK, cols)
        out_spec = pl.BlockSpec((tile // PACK, cols), lambda i: (i, 0))
    return pl.pallas_call(
        functools.partial(_pack_table_kernel, transpose=transpose),
        grid=(rows // tile,),
        in_specs=[pl.BlockSpec((tile, cols), lambda i: (i, 0))],
        out_specs=out_spec,
        out_shape=jax.ShapeDtypeStruct(out_shape, jnp.uint32),
        compiler_params=_params("parallel"),
        name="pack_table",
    )(x)


def peer_dense(xt, u, vt, sel, res, g, b, *, alpha, tt, te):
    d, t = xt.shape
    n_exp = u.shape[0] * PACK
    n_tiles = n_exp // te
    assert n_tiles % 2 == 0
    n_pairs = n_tiles // 2
    nchunk = tt // LANES
    sel_specs = [pl.BlockSpec((nchunk,) + a.shape[1:], lambda ti, s: (ti, 0, 0, 0)) for a in sel]
    full = lambda shape: pl.BlockSpec(shape, lambda ti, s: (0, 0))
    u_spec = pl.BlockSpec((2 * te // PACK, d), lambda ti, s: (jnp.minimum(s, n_pairs - 1), 0))
    vt_spec = pl.BlockSpec((d // PACK, 2 * te), lambda ti, s: (0, jnp.maximum(s - 1, 0)))
    return pl.pallas_call(
        functools.partial(_peer_dense_kernel, alpha=alpha, te=te, tt=tt, n_tiles=n_tiles),
        grid=(t // tt, n_pairs + 1),
        in_specs=[pl.BlockSpec((d, tt), lambda ti, s: (0, ti)), u_spec, vt_spec,
                  *sel_specs,
                  pl.BlockSpec((tt, d), lambda ti, s: (ti, 0)),
                  full((1, d)), full((1, d))],
        out_specs=pl.BlockSpec((tt, d), lambda ti, s: (ti, 0)),
        out_shape=jax.ShapeDtypeStruct((t, d), F32),
        scratch_shapes=[pltpu.VMEM((te, tt), F32), pltpu.VMEM((te, tt), F32),
                        pltpu.VMEM((te, tt), BF16), pltpu.VMEM((te, tt), BF16),
                        pltpu.VMEM((d, tt), F32)],
        compiler_params=_params("parallel", "arbitrary"),
        name="peer_dense",
    )(xt, u, vt, *sel, res, g[None, :], b[None, :])


def _ple_kernel(h_ref, p_ref, wg_ref, wp_ref, o_ref):
    h = h_ref[...]
    gate = jax.nn.sigmoid(jnp.dot(h.astype(BF16), wg_ref[...], preferred_element_type=F32))
    proj = jnp.dot(p_ref[...].astype(BF16), wp_ref[...], preferred_element_type=F32)
    o_ref[...] = h + gate * proj


def ple_add(h, p, wg, wp, *, tm):
    m, d = h.shape
    pd = p.shape[1]
    rows = lambda width: pl.BlockSpec((tm, width), lambda i: (i, 0))
    full = lambda shape: pl.BlockSpec(shape, lambda i: (0, 0))
    return pl.pallas_call(
        _ple_kernel,
        grid=(m // tm,),
        in_specs=[rows(d), rows(pd), full((d, d)), full((pd, d))],
        out_specs=rows(d),
        out_shape=jax.ShapeDtypeStruct((m, d), F32),
        compiler_params=_params("parallel"),
        name="ple_add",
    )(h, p, wg, wp)


TOKEN_TILE = 512
PEER_EXPERT_TILE = 512


def _largest_tile(n, candidates):
    for c in candidates:
        if n % c == 0:
            return c
    raise ValueError(f"no tile in {candidates} divides {n}")


def kernel(x_prompt, x_sample, cache_k, cache_v, state_conv, page_table, p_prompt, p_sample,
           ln_g, ln_b, w_attn_qkv, w_attn_lambda, attn_subln_g, w_attn_o,
           w_conv_in, conv_w, w_conv_out, w_peer_q, peer_keys, peer_u, peer_v,
           w_ple_gate, w_ple_proj):
    batch, seq, d = x_prompt.shape
    bs = x_sample.shape[0]
    assert x_sample.shape[1] == 1
    depth = ln_g.shape[0]
    tp = batch * seq
    t = tp + bs
    tm = TOKEN_TILE
    t_pad = -(-t // tm) * tm
    alpha = (2 * depth) ** 0.25
    width = ATT_HEADS * ATT_W
    tseq = _largest_tile(seq, (512, 256, 128))

    def joint(prompt_rows, sample_rows):
        pad = jnp.zeros((t_pad - t, prompt_rows.shape[1]), F32)
        return jnp.concatenate([prompt_rows, sample_rows, pad], axis=0)

    h = joint(x_prompt.reshape(tp, d), x_sample.reshape(bs, d))
    slopes = jnp.exp2(-8.0 * jnp.arange(1, ATT_HEADS + 1, dtype=F32) / ATT_HEADS)
    slopes_b = jnp.broadcast_to(slopes[:, None, None], (ATT_HEADS, 1, tseq))

    kp_l, vp_l, ks_l, vs_l, cp_l, cs_l = [], [], [], [], [], []
    for i in range(depth):
        j = i // N_MIXERS
        if i % N_MIXERS == 0:
            lam_init = 0.8 - 0.6 * math.exp(-0.3 * i)
            qkv = matmul_slabs(h, w_attn_qkv[j].astype(BF16), tm=tm, tn=width)
            g_sub = attn_subln_g[j]
            o_p = flash_prompt(qkv, slopes_b, w_attn_lambda[j], g_sub[None, :], batch=batch, seq=seq,
                               tq=tseq, tk=tseq, lam_init=lam_init)
            qkv_s = qkv[:, tp:t].reshape(3, bs, ATT_HEADS, ATT_W)
            o_s = decode_sample(page_table, qkv_s[0], qkv_s[1], qkv_s[2], cache_k, cache_v,
                                w_attn_lambda[j], g_sub, layer=j, lam_init=lam_init)
            o = joint(o_p, o_s.reshape(bs, width))
            h = mm_res_ln(o, w_attn_o[j].astype(BF16), h, ln_g[i, 0], ln_b[i, 0], alpha=alpha, tm=tm)
            kp_l.append(qkv[1, :tp].reshape(batch, seq, ATT_HEADS, ATT_W))
            vp_l.append(qkv[2, :tp].reshape(batch, seq, ATT_HEADS, ATT_W))
            ks_l.append(qkv_s[1].reshape(bs, 1, ATT_HEADS, ATT_W))
            vs_l.append(qkv_s[2].reshape(bs, 1, ATT_HEADS, ATT_W))
        else:
            w_in = w_conv_in[j].astype(BF16)
            w_out = w_conv_out[j].astype(BF16)
            hp, tail = conv_prompt(h[:tp], w_in, conv_w[j], w_out, ln_g[i, 0], ln_b[i, 0],
                                   batch=batch, seq=seq, alpha=alpha, tm=tseq)
            left = state_conv[j]
            hs, u_s = conv_sample(h[tp:t], left[:, 0], left[:, 1], w_in, conv_w[j], w_out,
                                  ln_g[i, 0], ln_b[i, 0], alpha=alpha)
            h = joint(hp, hs)
            cp_l.append(tail[:, CARRY_ROWS - (CONV_W - 1):])
            cs_l.append(jnp.stack([left[:, 1], u_s], axis=1))

        n_hc = 2 * PEER_HEADS
        dh = peer_keys.shape[-1]
        q_slabs = matmul_slabs(h, w_peer_q[i].astype(BF16), tm=tm, tn=d)
        keys = peer_keys[i].reshape(n_hc, N_KEYS, dh).astype(BF16)
        sel = peer_select(q_slabs, keys)
        u_packed = pack_table(peer_u[i], transpose=False)
        vt_packed = pack_table(peer_v[i], transpose=True)
        h = peer_dense(h.T.astype(BF16), u_packed, vt_packed, sel, h,
                       ln_g[i, 1], ln_b[i, 1], alpha=alpha, tt=tm, te=PEER_EXPERT_TILE)

        p = joint(p_prompt[i].reshape(tp, -1), p_sample[i].reshape(bs, -1))
        h = ple_add(h, p, w_ple_gate[i].astype(BF16), w_ple_proj[i].astype(BF16), tm=tm)

    return (h[:tp].reshape(batch, seq, d), h[tp:t].reshape(bs, 1, d),
            jnp.stack(kp_l), jnp.stack(vp_l), jnp.stack(ks_l), jnp.stack(vs_l),
            jnp.stack(cp_l), jnp.stack(cs_l))
```

```python
import functools
import math

import jax
import jax.numpy as jnp
from jax import lax
from jax.experimental import pallas as pl
from jax.experimental.pallas import tpu as pltpu

BF16 = jnp.bfloat16
F32 = jnp.float32

ATT_HEADS = 8
ATT_HD = 64
ATT_W = 2 * ATT_HD
N_MIXERS = 2
CONV_W = 3
PEER_HEADS = 8
N_KEYS = 128
PEER_TOPK = 16
LN_EPS = 1e-5
SUBLN_EPS = 1e-5
NEG_INF = -1e30
LOG2E = math.log2(math.e)
LANES = 128
SUBLANES = 8
VMEM_LIMIT = 56 * 1024 * 1024


def _params(*sem, flags=None):
    return pltpu.CompilerParams(dimension_semantics=sem, vmem_limit_bytes=VMEM_LIMIT, flags=flags)


def _nt_dot(a, b):
    return lax.dot_general(a, b, (((1,), (1,)), ((), ())), preferred_element_type=F32)


def _layer_norm(z, g, b):
    mu = jnp.mean(z, axis=-1, keepdims=True)
    zc = z - mu
    var = jnp.mean(zc * zc, axis=-1, keepdims=True)
    return zc * lax.rsqrt(var + LN_EPS) * g + b


def _diff_lambda(wl, lam_init):
    a = jnp.sum(wl[0:1] * wl[1:2], axis=1, keepdims=True)
    b = jnp.sum(wl[2:3] * wl[3:4], axis=1, keepdims=True)
    return jnp.exp(a) - jnp.exp(b) + lam_init


def _mm_kernel(x_ref, w_ref, o_ref):
    o_ref[...] = jnp.dot(x_ref[...].astype(BF16), w_ref[...], preferred_element_type=F32)


def matmul_slabs(x, w, *, tm, tn):
    m, k = x.shape
    n = w.shape[1]
    return pl.pallas_call(
        _mm_kernel,
        grid=(m // tm, n // tn),
        in_specs=[pl.BlockSpec((tm, k), lambda i, j: (i, 0)),
                  pl.BlockSpec((k, tn), lambda i, j: (0, j))],
        out_specs=pl.BlockSpec((None, tm, tn), lambda i, j: (j, i, 0)),
        out_shape=jax.ShapeDtypeStruct((n // tn, m, tn), F32),
        compiler_params=_params("parallel", "parallel"),
        name="matmul_slabs",
    )(x, w)


def _qkv_heads_kernel(x_ref, w_ref, o_ref, kh_ref, vh_ref):
    j = pl.program_id(1)
    y = jnp.dot(x_ref[...].astype(BF16), w_ref[...], preferred_element_type=F32)
    o_ref[...] = y

    def heads_out(ref):
        for h in range(ATT_HEADS):
            ref[:, h, :] = y[:, h * ATT_W:(h + 1) * ATT_W]

    pl.when(j == 1)(functools.partial(heads_out, kh_ref))
    pl.when(j == 2)(functools.partial(heads_out, vh_ref))


def qkv_project(x, w, *, tm):
    m, k = x.shape
    width = ATT_HEADS * ATT_W
    heads = jax.ShapeDtypeStruct((m, ATT_HEADS, ATT_W), F32)
    heads_spec = pl.BlockSpec((tm, ATT_HEADS, ATT_W), lambda i, j: (i, 0, 0))
    return pl.pallas_call(
        _qkv_heads_kernel,
        grid=(m // tm, 3),
        in_specs=[pl.BlockSpec((tm, k), lambda i, j: (i, 0)),
                  pl.BlockSpec((k, width), lambda i, j: (0, j))],
        out_specs=[pl.BlockSpec((None, tm, width), lambda i, j: (j, i, 0)), heads_spec, heads_spec],
        out_shape=[jax.ShapeDtypeStruct((3, m, width), F32), heads, heads],
        compiler_params=_params("parallel", "arbitrary"),
        name="qkv_project",
    )(x, w)


FLASH_ROW_GROUPS = 4


def _flash_kernel(slope_ref, wl_ref, g_ref, q_ref, k_ref, v_ref, o_ref,
                  q_s, m_s, l_s, a_s, *, tq, tk, lam_init):
    qi = pl.program_id(2)
    ki = pl.program_id(3)
    nk = pl.num_programs(3)
    q0 = qi * tq
    k0 = ki * tk

    @pl.when(ki == 0)
    def _init():
        q = q_ref[...] * (ATT_HD ** -0.5 * LOG2E)
        lane = lax.broadcasted_iota(jnp.int32, q.shape, 1)
        q_s[0:tq] = jnp.where(lane < ATT_HD, q, 0.0).astype(BF16)
        q_s[tq:2 * tq] = jnp.where(lane >= ATT_HD, q, 0.0).astype(BF16)
        m_s[...] = jnp.full(m_s.shape, NEG_INF, F32)
        l_s[...] = jnp.zeros(l_s.shape, F32)
        a_s[...] = jnp.zeros(a_s.shape, F32)

    def step(on_diagonal):
        kb = k_ref[...].astype(BF16)
        vb = jnp.concatenate([v_ref[...].astype(BF16), jnp.ones((tk, LANES), BF16)], axis=1)
        col = lax.broadcasted_iota(jnp.int32, (1, tk), 1)
        bias = (slope_ref[...] * LOG2E) * (k0 + col - q0).astype(F32)
        rg = 2 * tq // FLASH_ROW_GROUPS
        groups = [slice(g * rg, (g + 1) * rg) for g in range(FLASH_ROW_GROUPS)]
        scores = [_nt_dot(q_s[rows, :], kb) for rows in groups]
        for g, rows in enumerate(groups):
            s = scores[g] + bias
            if on_diagonal:
                r = lax.broadcasted_iota(jnp.int32, (rg, tk), 0) + (g * rg) % tq
                c = lax.broadcasted_iota(jnp.int32, (rg, tk), 1)
                s = jnp.where(c > r, NEG_INF, s)
            m_old = m_s[rows, :]
            m_new = jnp.maximum(m_old, jnp.max(s, axis=1, keepdims=True))
            alpha = jnp.exp2(m_old - m_new)
            p = jnp.exp2(s - jnp.tile(m_new, (1, tk // LANES)))
            pv = jnp.dot(p.astype(BF16), vb, preferred_element_type=F32)
            l_s[rows, :] = alpha * l_s[rows, :] + pv[:, ATT_W:]
            a_s[rows, :] = alpha * a_s[rows, :] + pv[:, :ATT_W]
            m_s[rows, :] = m_new

    pl.when(ki < qi)(functools.partial(step, False))
    pl.when(ki == qi)(functools.partial(step, True))

    @pl.when(ki == nk - 1)
    def _finish():
        lam = _diff_lambda(wl_ref[...], lam_init)
        w = a_s[...] / l_s[...]
        o = w[0:tq] - lam * w[tq:2 * tq]
        ms = jnp.mean(o * o, axis=1, keepdims=True)
        o_ref[...] = o * lax.rsqrt(ms + SUBLN_EPS) * g_ref[...] * (1.0 - lam_init)


def flash_prompt(qkv, slopes, w_lam, subln_g, *, batch, seq, tq, tk, lam_init):
    assert tq == tk
    nq = seq // tq
    nkb = seq // tk

    def kv_row(b, qi, ki):
        return b * nkb + jnp.minimum(ki, qi)

    kern = functools.partial(_flash_kernel, tq=tq, tk=tk, lam_init=lam_init)
    return pl.pallas_call(
        kern,
        grid=(batch, ATT_HEADS, nq, nkb),
        in_specs=[
            pl.BlockSpec((None, 1, tk), lambda b, h, qi, ki: (h, 0, 0)),
            pl.BlockSpec((4, ATT_HD), lambda b, h, qi, ki: (0, 0)),
            pl.BlockSpec((1, ATT_W), lambda b, h, qi, ki: (0, 0)),
            pl.BlockSpec((None, tq, ATT_W), lambda b, h, qi, ki: (0, b * nq + qi, h)),
            pl.BlockSpec((None, tk, ATT_W), lambda b, h, qi, ki: (1, kv_row(b, qi, ki), h)),
            pl.BlockSpec((None, tk, ATT_W), lambda b, h, qi, ki: (2, kv_row(b, qi, ki), h)),
        ],
        out_specs=pl.BlockSpec((tq, ATT_W), lambda b, h, qi, ki: (b * nq + qi, h)),
        out_shape=jax.ShapeDtypeStruct((batch * seq, ATT_HEADS * ATT_W), F32),
        scratch_shapes=[
            pltpu.VMEM((2 * tq, ATT_W), BF16), pltpu.VMEM((2 * tq, LANES), F32),
            pltpu.VMEM((2 * tq, LANES), F32), pltpu.VMEM((2 * tq, ATT_W), F32),
        ],
        compiler_params=_params("parallel", "parallel", "parallel", "arbitrary"),
        name="flash_prompt",
    )(slopes, w_lam, subln_g, qkv, qkv, qkv)


DECODE_PAGES_PER_STEP = 4


def _decode_kernel(pt_ref, q_ref, kn_ref, vn_ref, *rest, n_grp, page, past, lam_init):
    del pt_ref
    k_refs = rest[:n_grp]
    v_refs = rest[n_grp:2 * n_grp]
    bsel_ref, slope_ref, alibi_ref, g_ref, wl_ref, o_ref, m_s, l_s, aa_s, ab_s = rest[2 * n_grp:]
    pg = pl.program_id(1)
    n_steps = pl.num_programs(1)
    nh = ATT_HEADS
    q8 = q_ref[...] * (ATT_HD ** -0.5 * LOG2E)

    def half_sums(prod):
        return jnp.dot(prod.astype(BF16), bsel_ref[...], preferred_element_type=F32)

    def swap_halves(x):
        return pltpu.roll(x, ATT_HD, x.ndim - 1)

    @pl.when(pg == 0)
    def _init():
        m_s[...] = jnp.full(m_s.shape, NEG_INF, F32)
        for ref in (l_s, aa_s, ab_s):
            ref[...] = jnp.zeros(ref.shape, F32)

    slope = slope_ref[...]
    logits, shifts = [], []
    m_new = m_s[...]
    for g in range(n_grp):
        prod = (k_refs[g][...] * q8[None]).reshape(page * nh, ATT_W)
        s3 = half_sums(prod).reshape(page, nh, ATT_W) + alibi_ref[...]
        shift = slope * (past - (pg * n_grp + g) * page).astype(F32)
        m_new = jnp.maximum(m_new, jnp.max(s3, axis=0) - shift)
        logits.append(s3)
        shifts.append(shift)
    alpha = jnp.exp2(m_s[...] - m_new)
    l = alpha * l_s[...]
    acc_a = alpha * aa_s[...]
    acc_b = swap_halves(alpha) * ab_s[...]
    for g in range(n_grp):
        pe = jnp.exp2(logits[g] - (m_new + shifts[g])[None])
        v3 = v_refs[g][...]
        l = l + jnp.sum(pe, axis=0)
        acc_a = acc_a + jnp.sum(pe * v3, axis=0)
        acc_b = acc_b + jnp.sum(swap_halves(pe) * v3, axis=0)
    m_s[...] = m_new
    l_s[...] = l
    aa_s[...] = acc_a
    ab_s[...] = acc_b

    @pl.when(pg == n_steps - 1)
    def _finish():
        s_self = half_sums(q8 * kn_ref[...])
        m_n = jnp.maximum(m_new, s_self)
        al = jnp.exp2(m_new - m_n)
        p_self = jnp.exp2(s_self - m_n)
        lf = al * l + p_self
        vn = vn_ref[...]
        fa = al * acc_a + p_self * vn
        fb = swap_halves(al) * acc_b + swap_halves(p_self) * vn
        first = lax.broadcasted_iota(jnp.int32, fa.shape, 1) < ATT_HD
        lf_sw = swap_halves(lf)
        o1 = jnp.where(first, fa, fb) / jnp.where(first, lf, lf_sw)
        o2 = jnp.where(first, fb, fa) / jnp.where(first, lf_sw, lf)
        lam = _diff_lambda(wl_ref[...], lam_init)
        d = o1 - lam * o2
        ms = jnp.mean(d * d, axis=1, keepdims=True)
        o_ref[...] = d * lax.rsqrt(ms + SUBLN_EPS) * g_ref[...] * (1.0 - lam_init)


def decode_sample(page_table, q, k_new, v_new, cache_k, cache_v, w_lam, subln_g, *, layer, lam_init):
    bs, n_pages = page_table.shape
    page = cache_k.shape[2]
    nh = ATT_HEADS
    past = n_pages * page
    n_grp = math.gcd(n_pages, DECODE_PAGES_PER_STEP)
    half = jnp.arange(ATT_W)[:, None] // ATT_HD == jnp.arange(ATT_W)[None, :] // ATT_HD
    bsel = half.astype(BF16)
    slopes = jnp.exp2(-8.0 * jnp.arange(1, nh + 1, dtype=F32) / nh) * LOG2E
    slope = jnp.broadcast_to(slopes[:, None], (nh, ATT_W))
    alibi = jnp.arange(page, dtype=F32)[:, None, None] * slope[None]

    row_spec = pl.BlockSpec((None, nh, ATT_W), lambda b, p, pt: (b, 0, 0))
    const = lambda shape: pl.BlockSpec(shape, lambda b, p, pt: (0,) * len(shape))

    def page_spec(g):
        return pl.BlockSpec((None, None, page, nh, ATT_W),
                            lambda b, p, pt: (layer, pt[b, p * n_grp + g], 0, 0, 0))

    pages = [page_spec(g) for g in range(n_grp)]
    kern = functools.partial(_decode_kernel, n_grp=n_grp, page=page, past=past, lam_init=lam_init)
    return pl.pallas_call(
        kern,
        grid_spec=pltpu.PrefetchScalarGridSpec(
            num_scalar_prefetch=1,
            grid=(bs, n_pages // n_grp),
            in_specs=[row_spec, row_spec, row_spec] + pages + pages + [
                const((ATT_W, ATT_W)), const((nh, ATT_W)), const((page, nh, ATT_W)),
                const((1, ATT_W)), const((4, ATT_HD))],
            out_specs=pl.BlockSpec((None, nh, ATT_W), lambda b, p, pt: (b, 0, 0)),
            scratch_shapes=[pltpu.VMEM((nh, ATT_W), F32)] * 4,
        ),
        out_shape=jax.ShapeDtypeStruct((bs, nh, ATT_W), F32),
        compiler_params=_params("parallel", "arbitrary"),
        name="decode_sample",
    )(page_table, q, k_new, v_new, *([cache_k] * n_grp), *([cache_v] * n_grp),
      bsel, slope, alibi, subln_g[None, :], w_lam)


def _mm_res_ln_kernel(xp_ref, xt_ref, w_ref, rp_ref, rt_ref, g_ref, b_ref, o_ref, *, alpha, n_prompt_tiles):
    tail = pl.program_id(0) >= n_prompt_tiles
    x = jnp.where(tail, xt_ref[...], xp_ref[...])
    res = jnp.where(tail, rt_ref[...], rp_ref[...])
    y = jnp.dot(x.astype(BF16), w_ref[...], preferred_element_type=F32)
    o_ref[...] = _layer_norm(alpha * res + y, g_ref[...], b_ref[...])


def mm_res_ln(x_prompt, x_tail, w, res_prompt, res_tail, g, b, *, alpha, tm):
    tp, k = x_prompt.shape
    n = w.shape[1]
    n_p = tp // tm
    prompt = lambda width: pl.BlockSpec((tm, width), lambda i: (jnp.minimum(i, n_p - 1), 0))
    full = lambda shape: pl.BlockSpec(shape, lambda i: (0, 0))
    return pl.pallas_call(
        functools.partial(_mm_res_ln_kernel, alpha=alpha, n_prompt_tiles=n_p),
        grid=(n_p + 1,),
        in_specs=[prompt(k), full((tm, k)), full((k, n)), prompt(n), full((tm, n)),
                  full((1, n)), full((1, n))],
        out_specs=pl.BlockSpec((tm, n), lambda i: (i, 0)),
        out_shape=jax.ShapeDtypeStruct((tp + tm, n), F32),
        compiler_params=_params("parallel"),
        name="mm_res_ln",
    )(x_prompt, x_tail, w, res_prompt, res_tail, g[None, :], b[None, :])


CARRY_ROWS = 8


def _conv_kernel(*refs, alpha, tm, chained):
    if chained:
        x_ref, win_ref, cw_ref, wout_ref, g_ref, b_ref, o_ref, u_ref, carry_s = refs
    else:
        x_ref, l0_ref, l1_ref, win_ref, cw_ref, wout_ref, g_ref, b_ref, joint_ref, o_ref, u_ref = refs
        del joint_ref
    d = x_ref.shape[1]
    x = x_ref[...]
    bch = jnp.dot(x.astype(BF16), win_ref[...], preferred_element_type=F32)
    b_g = bch[:, 0:d]
    u = bch[:, d:2 * d] * bch[:, 2 * d:3 * d]
    cw = cw_ref[...]
    if chained:
        i = pl.program_id(1)

        @pl.when(i == 0)
        def _zero_left():
            carry_s[...] = jnp.zeros(carry_s.shape, F32)

        prev = carry_s[...]
        row = lax.broadcasted_iota(jnp.int32, u.shape, 0)
        last = prev[CARRY_ROWS - 1:CARRY_ROWS]
        u1 = jnp.where(row == 0, last, pltpu.roll(u, 1, 0))
        u2 = jnp.where(row == 0, prev[CARRY_ROWS - 2:CARRY_ROWS - 1],
                       jnp.where(row == 1, last, pltpu.roll(u, 2, 0)))
        tail = u[tm - CARRY_ROWS:tm]
        carry_s[...] = tail
        u_ref[...] = tail
    else:
        u2 = l0_ref[...]
        u1 = l1_ref[...]
        u_ref[...] = u
    z = cw[0:1] * u2 + cw[1:2] * u1 + cw[2:3] * u
    y = jnp.dot((b_g * z).astype(BF16), wout_ref[...], preferred_element_type=F32)
    o_ref[...] = _layer_norm(alpha * x + y, g_ref[...], b_ref[...])


def conv_prompt(x, w_in, conv_w, w_out, g, b, *, batch, seq, alpha, tm):
    t_rows, d = x.shape
    nt = seq // tm
    rows = pl.BlockSpec((tm, d), lambda bi, i: (bi * nt + i, 0))
    full = lambda shape: pl.BlockSpec(shape, lambda bi, i: (0, 0))
    return pl.pallas_call(
        functools.partial(_conv_kernel, alpha=alpha, tm=tm, chained=True),
        grid=(batch, nt),
        in_specs=[rows, full((d, 3 * d)), full((CONV_W, d)), full((d, d)), full((1, d)), full((1, d))],
        out_specs=[rows, pl.BlockSpec((None, CARRY_ROWS, d), lambda bi, i: (bi, 0, 0))],
        out_shape=[jax.ShapeDtypeStruct((t_rows, d), F32),
                   jax.ShapeDtypeStruct((batch, CARRY_ROWS, d), F32)],
        scratch_shapes=[pltpu.VMEM((CARRY_ROWS, d), F32)],
        compiler_params=_params("parallel", "arbitrary"),
        name="conv_prompt",
    )(x, w_in, conv_w, w_out, g[None, :], b[None, :])


def conv_sample(x, joint_out, left0, left1, w_in, conv_w, w_out, g, b, *, alpha, tm):
    t_rows, d = x.shape
    last = t_rows // tm - 1
    tile = pl.BlockSpec((tm, d), lambda i: (last, 0))
    full = lambda shape: pl.BlockSpec(shape, lambda i: (0, 0))
    return pl.pallas_call(
        functools.partial(_conv_kernel, alpha=alpha, tm=tm, chained=False),
        grid=(1,),
        in_specs=[tile, full((tm, d)), full((tm, d)), full((d, 3 * d)), full((CONV_W, d)),
                  full((d, d)), full((1, d)), full((1, d)), pl.BlockSpec(memory_space=pl.ANY)],
        out_specs=[tile, full((tm, d))],
        out_shape=[jax.ShapeDtypeStruct((t_rows, d), F32), jax.ShapeDtypeStruct((tm, d), F32)],
        input_output_aliases={8: 0},
        compiler_params=_params("arbitrary"),
        name="conv_sample",
    )(x, left0, left1, w_in, conv_w, w_out, g[None, :], b[None, :], joint_out)


PACK = 4 // jnp.dtype(BF16).itemsize


def _pack_rows(x):
    return pltpu.bitcast(x.astype(BF16), jnp.uint32)


def _unpack_rows(x):
    return pltpu.bitcast(x, BF16)


def _replicate_word(x):
    if PACK == 1:
        return pltpu.bitcast(x, jnp.uint32)
    hi = pltpu.bitcast(x.astype(BF16).astype(F32), jnp.uint32)
    return hi | (hi >> 16)


def _top_values(x, count, store, want_rank=False):
    rank = jnp.full(x.shape, float(count), F32) if want_rank else None
    for r in range(count):
        mx = jnp.max(x, axis=0, keepdims=True)
        store(r, mx)
        hit = x == mx
        if want_rank:
            rank = jnp.where(hit, float(r), rank)
        if r + 1 < count:
            x = jnp.where(hit, -jnp.inf, x)
    return rank


def _select_kernel(q_ref, keys_ref, cnt_ref, e1_ref, r2_ref, e2_ref, s1_s, sv_s):
    k = PEER_TOPK
    dh = keys_ref.shape[2]
    per_slab = q_ref.shape[2] // dh
    for hc in range(2 * PEER_HEADS):
        h, second = divmod(hc, 2)
        qb = q_ref[hc // per_slab, :, (hc % per_slab) * dh:(hc % per_slab + 1) * dh].astype(BF16)
        s = _nt_dot(keys_ref[hc], qb)

        def store(r, mx, hc=hc):
            sv_s[hc, r:r + 1, :] = mx

        rank = _top_values(s, k, store, want_rank=bool(second))
        if second:
            r2_ref[0, h] = _pack_rows(rank)
            e2_ref[0, h] = _pack_rows(jnp.exp(s - sv_s[hc, 0:1, :]))
        else:
            s1_s[h] = s

    for h in range(PEER_HEADS):
        sv1 = sv_s[2 * h]
        sv2 = sv_s[2 * h + 1]
        sub = lax.broadcasted_iota(jnp.int32, (SUBLANES, LANES), 0)
        pieces = [sv1[0:1] + sv2]
        for a in range(2, SUBLANES + 1):
            sums = sv1[a - 1:a] + sv2[0:SUBLANES]
            pieces.append(sums if k // a >= SUBLANES else jnp.where(sub < k // a, sums, -jnp.inf))
        pieces.append(sv1[SUBLANES:k] + sv2[0:1])
        cand = jnp.concatenate(pieces, axis=0)
        tau_box = []
        _top_values(cand, k, lambda r, mx: tau_box.append(mx))
        tau = tau_box[-1]
        top = sv1[0:1] + sv2[0:1]
        z = jnp.sum(jnp.where(cand >= tau, jnp.exp(cand - top), 0.0), axis=0, keepdims=True)
        s1 = s1_s[h]
        cnt = jnp.zeros(s1.shape, F32)
        for b in range(k // 2):
            cnt = cnt + jnp.where(s1 + sv2[b:b + 1] >= tau, 1.0, 0.0)
        cnt_best = jnp.zeros((1, LANES), F32)
        for b in range(k // 2, k):
            cnt_best = cnt_best + jnp.where(sv1[0:1] + sv2[b:b + 1] >= tau, 1.0, 0.0)
        cnt_ref[0, h] = _replicate_word(jnp.where(s1 == sv1[0:1], cnt + cnt_best, cnt))
        e1_ref[0, h] = _replicate_word(jnp.exp(s1 - sv1[0:1]) * (1.0 / z))


def peer_select(q_slabs, keys):
    n_slab, t, slab_w = q_slabs.shape
    nhc, _, dh = keys.shape
    nchunk = t // LANES
    def out(rows, dtype):
        spec = pl.BlockSpec((1, PEER_HEADS, rows, LANES), lambda i: (i, 0, 0, 0))
        return spec, jax.ShapeDtypeStruct((nchunk, PEER_HEADS, rows, LANES), dtype)

    outs = [out(N_KEYS, jnp.uint32), out(N_KEYS, jnp.uint32),
            out(N_KEYS // PACK, jnp.uint32), out(N_KEYS // PACK, jnp.uint32)]
    return pl.pallas_call(
        _select_kernel,
        grid=(nchunk,),
        in_specs=[pl.BlockSpec((n_slab, LANES, slab_w), lambda i: (0, i, 0)),
                  pl.BlockSpec((nhc, N_KEYS, dh), lambda i: (0, 0, 0))],
        out_specs=[spec for spec, _ in outs],
        out_shape=[sds for _, sds in outs],
        scratch_shapes=[pltpu.VMEM((PEER_HEADS, N_KEYS, LANES), F32),
                        pltpu.VMEM((nhc, PEER_TOPK, LANES), F32)],
        compiler_params=_params("parallel"),
        name="peer_select",
    )(q_slabs, keys)


GATE_ROWS = 16
MXU_PIECES = 4
REGIONS_PER_HALF = 1


def _gelu(a):
    return 0.5 * a * (1.0 + lax.erf(a * (2.0 ** -0.5)))


def _peer_dense_kernel(xt_ref, u_ref, vt_ref, cnt_ref, e1_ref, r2_ref, e2_ref, res_ref,
                       g_ref, b_ref, o_ref, a0_s, a1_s, w0_s, w1_s, acc_s, *, alpha, te, tt, n_tiles):
    gstep = pl.program_id(1)
    n_steps = pl.num_programs(1)
    n_i = te // N_KEYS

    @pl.when(gstep == 0)
    def _init():
        for ref in (a0_s, a1_s, w0_s, w1_s, acc_s):
            ref[...] = jnp.zeros(ref.shape, ref.dtype)

    assert n_i == 4 and tt % (2 * LANES) == 0
    th = tt // 2

    def gate_block(iis, tc, rbs, i0, a_ref, w_ref):
        lanes = slice(tc * LANES, (tc + 1) * LANES)

        def row(ref, ii, h):
            r = ref[tc, h, pl.ds(i0 + ii, 1), :]
            return _unpack_rows(jnp.broadcast_to(r, (GATE_ROWS // PACK, LANES)))

        cnt = {(ii, h): row(cnt_ref, ii, h) for ii in iis for h in range(PEER_HEADS)}
        e1 = {(ii, h): row(e1_ref, ii, h) for ii in iis for h in range(PEER_HEADS)}
        zero = jnp.zeros((GATE_ROWS, LANES), BF16)
        for rb in rbs:
            keys = slice(rb * GATE_ROWS // PACK, (rb + 1) * GATE_ROWS // PACK)
            gate = {ii: zero for ii in iis}
            for h in range(PEER_HEADS):
                r2 = _unpack_rows(r2_ref[tc, h, keys, :])
                e2 = _unpack_rows(e2_ref[tc, h, keys, :])
                for ii in iis:
                    gate[ii] = gate[ii] + jnp.where(r2 < cnt[ii, h], e2, zero) * e1[ii, h]
            for ii in iis:
                out_rows = slice(ii * N_KEYS + rb * GATE_ROWS, ii * N_KEYS + (rb + 1) * GATE_ROWS)
                w_ref[out_rows, lanes] = gate[ii] * _gelu(a_ref[out_rows, lanes].astype(BF16))

    def half_step(tile, a_src, w_dst, w_src, vt0, a_dst, u0, region_base):
        i0 = jnp.clip(tile, 0, n_tiles - 1) * n_i
        n_tc = tt // LANES
        def quarter(r):
            iis = (2 * (r // 2), 2 * (r // 2) + 1)
            n_rb = N_KEYS // GATE_ROWS
            units = [(tc, range(part * n_rb // 2, (part + 1) * n_rb // 2))
                     for tc in range((r % 2) * n_tc // 2, (r % 2 + 1) * n_tc // 2) for part in range(2)]
            for piece in range(MXU_PIECES):
                if r < 2:
                    tok = slice(r * th, (r + 1) * th)
                    rows = slice(piece * (d // MXU_PIECES), (piece + 1) * (d // MXU_PIECES))
                    vt = _unpack_rows(vt_ref[rows.start // PACK:rows.stop // PACK, vt0:vt0 + te])
                    acc_s[rows, tok] += jnp.dot(vt, w_src[:, tok], preferred_element_type=F32)
                else:
                    tok = slice((r - 2) * th, (r - 1) * th)
                    rows = slice(piece * (te // MXU_PIECES), (piece + 1) * (te // MXU_PIECES))
                    u = _unpack_rows(u_ref[(u0 + rows.start) // PACK:(u0 + rows.stop) // PACK, :])
                    a_dst[rows, tok] = jnp.dot(u, xt_ref[:, tok], preferred_element_type=F32)
                for tc, rbs in units[piece * len(units) // MXU_PIECES:(piece + 1) * len(units) // MXU_PIECES]:
                    gate_block(iis, tc, rbs, i0, a_src, w_dst)

        per_region = n_i // REGIONS_PER_HALF
        for region in range(REGIONS_PER_HALF):
            @pl.when(gstep < n_steps + region_base + region)
            def _region(region=region):
                for r in range(region * per_region, (region + 1) * per_region):
                    quarter(r)

    d = acc_s.shape[0]
    half_step(2 * gstep - 1, a1_s, w1_s, w0_s, 0, a0_s, 0, 0)
    half_step(2 * gstep, a0_s, w0_s, w1_s, te, a1_s, te, REGIONS_PER_HALF)

    @pl.when(gstep == n_steps - 1)
    def _finish():
        y = acc_s[...].T
        o_ref[...] = _layer_norm(alpha * res_ref[...] + y, g_ref[...], b_ref[...])


def _pack_table_kernel(x_ref, o_ref, *, transpose):
    x = x_ref[...]
    o_ref[...] = _pack_rows(x.T if transpose else x)


def pack_table(tables, layer, *, transpose, tile=1024):
    _, rows, cols = tables.shape
    if transpose:
        out_shape = (cols // PACK, rows)
        out_spec = pl.BlockSpec((cols // PACK, tile), lambda i: (0, i))
    else:
        out_shape = (rows // PACK, cols)
        out_spec = pl.BlockSpec((tile // PACK, cols), lambda i: (i, 0))
    return pl.pallas_call(
        functools.partial(_pack_table_kernel, transpose=transpose),
        grid=(rows // tile,),
        in_specs=[pl.BlockSpec((None, tile, cols), lambda i: (layer, i, 0))],
        out_specs=out_spec,
        out_shape=jax.ShapeDtypeStruct(out_shape, jnp.uint32),
        compiler_params=_params("parallel"),
        name="pack_table",
    )(tables)


def peer_dense(xt, u, vt, sel, res, g, b, *, alpha, tt, te):
    d, t = xt.shape
    n_exp = u.shape[0] * PACK
    n_tiles = n_exp // te
    assert n_tiles % 2 == 0
    n_pairs = n_tiles // 2
    nchunk = tt // LANES
    sel_specs = [pl.BlockSpec((nchunk,) + a.shape[1:], lambda ti, s: (ti, 0, 0, 0)) for a in sel]
    full = lambda shape: pl.BlockSpec(shape, lambda ti, s: (0, 0))
    u_spec = pl.BlockSpec((2 * te // PACK, d), lambda ti, s: (jnp.minimum(s, n_pairs - 1), 0))
    vt_spec = pl.BlockSpec((d // PACK, 2 * te), lambda ti, s: (0, jnp.maximum(s - 1, 0)))
    return pl.pallas_call(
        functools.partial(_peer_dense_kernel, alpha=alpha, te=te, tt=tt, n_tiles=n_tiles),
        grid=(t // tt, n_pairs + 1),
        in_specs=[pl.BlockSpec((d, tt), lambda ti, s: (0, ti)), u_spec, vt_spec,
                  *sel_specs,
                  pl.BlockSpec((tt, d), lambda ti, s: (ti, 0)),
                  full((1, d)), full((1, d))],
        out_specs=pl.BlockSpec((tt, d), lambda ti, s: (ti, 0)),
        out_shape=jax.ShapeDtypeStruct((t, d), F32),
        scratch_shapes=[pltpu.VMEM((te, tt), F32), pltpu.VMEM((te, tt), F32),
                        pltpu.VMEM((te, tt), BF16), pltpu.VMEM((te, tt), BF16),
                        pltpu.VMEM((d, tt), F32)],
        compiler_params=_params("parallel", "arbitrary"),
        name="peer_dense",
    )(xt, u, vt, *sel, res, g[None, :], b[None, :])


def _ple_kernel(h_ref, pp_ref, pt_ref, wg_ref, wp_ref, *o_refs, n_prompt_tiles):
    i = pl.program_id(0)
    tail = i >= n_prompt_tiles
    h = h_ref[...]
    p = jnp.where(tail, pt_ref[...], pp_ref[...])
    gate = jax.nn.sigmoid(jnp.dot(h.astype(BF16), wg_ref[...], preferred_element_type=F32))
    proj = jnp.dot(p.astype(BF16), wp_ref[...], preferred_element_type=F32)
    out = h + gate * proj
    if len(o_refs) == 1:
        o_refs[0][...] = out
    else:
        op_ref, ot_ref = o_refs

        @pl.when(jnp.logical_not(tail))
        def _():
            op_ref[...] = out

        @pl.when(tail)
        def _():
            ot_ref[...] = out


def ple_add(h, p_prompt, layer, p_tail, wg, wp, *, tm, split):
    m, d = h.shape
    pd = p_tail.shape[1]
    n_p = m // tm - 1
    prompt_tile = lambda i: jnp.minimum(i, n_p - 1)
    full = lambda shape: pl.BlockSpec(shape, lambda i: (0, 0))
    if split:
        out_specs = [pl.BlockSpec((tm, d), lambda i: (prompt_tile(i), 0)), full((tm, d))]
        out_shape = [jax.ShapeDtypeStruct((n_p * tm, d), F32), jax.ShapeDtypeStruct((tm, d), F32)]
    else:
        out_specs = pl.BlockSpec((tm, d), lambda i: (i, 0))
        out_shape = jax.ShapeDtypeStruct((m, d), F32)
    return pl.pallas_call(
        functools.partial(_ple_kernel, n_prompt_tiles=n_p),
        grid=(m // tm,),
        in_specs=[pl.BlockSpec((tm, d), lambda i: (i, 0)),
                  pl.BlockSpec((None, tm, pd), lambda i: (layer, prompt_tile(i), 0)),
                  full((tm, pd)), full((d, d)), full((pd, d))],
        out_specs=out_specs,
        out_shape=out_shape,
        compiler_params=_params("arbitrary"),
        name="ple_add",
    )(h, p_prompt, p_tail, wg, wp)


TOKEN_TILE = 512
PEER_EXPERT_TILE = 512


def _largest_tile(n, candidates):
    for c in candidates:
        if n % c == 0:
            return c
    raise ValueError(f"no tile in {candidates} divides {n}")


def kernel(x_prompt, x_sample, cache_k, cache_v, state_conv, page_table, p_prompt, p_sample,
           ln_g, ln_b, w_attn_qkv, w_attn_lambda, attn_subln_g, w_attn_o,
           w_conv_in, conv_w, w_conv_out, w_peer_q, peer_keys, peer_u, peer_v,
           w_ple_gate, w_ple_proj):
    batch, seq, d = x_prompt.shape
    bs = x_sample.shape[0]
    assert x_sample.shape[1] == 1
    depth = ln_g.shape[0]
    tp = batch * seq
    t = tp + bs
    tm = TOKEN_TILE
    t_pad = -(-t // tm) * tm
    alpha = (2 * depth) ** 0.25
    width = ATT_HEADS * ATT_W
    tseq = _largest_tile(seq, (512, 256, 128))

    assert tp % tm == 0 and t_pad == tp + tm

    def tail_tile(sample_rows):
        pad = jnp.zeros((tm - bs, sample_rows.shape[1]), F32)
        return jnp.concatenate([sample_rows, pad], axis=0)

    h = None
    slopes = jnp.exp2(-8.0 * jnp.arange(1, ATT_HEADS + 1, dtype=F32) / ATT_HEADS)
    slopes_b = jnp.broadcast_to(slopes[:, None, None], (ATT_HEADS, 1, tseq))

    kp_l, vp_l, ks_l, vs_l, cp_l, cs_l = [], [], [], [], [], []
    for i in range(depth):
        j = i // N_MIXERS
        if i % N_MIXERS == 0:
            lam_init = 0.8 - 0.6 * math.exp(-0.3 * i)
            w_qkv = w_attn_qkv[j].astype(BF16)
            rows_p = x_prompt.reshape(tp, d) if h is None else h[:tp]
            rows_s = x_sample.reshape(bs, d) if h is None else h[tp:t]
            qkv, k_heads, v_heads = qkv_project(rows_p, w_qkv, tm=tseq)
            qkv_s = matmul_slabs(rows_s, w_qkv, tm=bs, tn=width).reshape(3, bs, ATT_HEADS, ATT_W)
            g_sub = attn_subln_g[j]
            o_p = flash_prompt(qkv, slopes_b, w_attn_lambda[j], g_sub[None, :], batch=batch, seq=seq,
                               tq=tseq, tk=tseq, lam_init=lam_init)
            o_s = decode_sample(page_table, qkv_s[0], qkv_s[1], qkv_s[2], cache_k, cache_v,
                                w_attn_lambda[j], g_sub, layer=j, lam_init=lam_init)
            h = mm_res_ln(o_p, tail_tile(o_s.reshape(bs, width)), w_attn_o[j].astype(BF16),
                          rows_p, tail_tile(rows_s), ln_g[i, 0], ln_b[i, 0], alpha=alpha, tm=tm)
            kp_l.append(k_heads.reshape(batch, seq, ATT_HEADS, ATT_W))
            vp_l.append(v_heads.reshape(batch, seq, ATT_HEADS, ATT_W))
            ks_l.append(qkv_s[1].reshape(bs, 1, ATT_HEADS, ATT_W))
            vs_l.append(qkv_s[2].reshape(bs, 1, ATT_HEADS, ATT_W))
        else:
            if h is None:
                h = jnp.concatenate([x_prompt.reshape(tp, d), tail_tile(x_sample.reshape(bs, d))], axis=0)
            w_in = w_conv_in[j].astype(BF16)
            w_out = w_conv_out[j].astype(BF16)
            h_new, tail = conv_prompt(h, w_in, conv_w[j], w_out, ln_g[i, 0], ln_b[i, 0],
                                      batch=batch, seq=seq, alpha=alpha, tm=tseq)
            left = state_conv[j]
            h, u_tile = conv_sample(h, h_new, tail_tile(left[:, 0]), tail_tile(left[:, 1]), w_in,
                                    conv_w[j], w_out, ln_g[i, 0], ln_b[i, 0], alpha=alpha, tm=tm)
            cp_l.append(tail[:, CARRY_ROWS - (CONV_W - 1):])
            cs_l.append(jnp.stack([left[:, 1], u_tile[:bs]], axis=1))

        n_hc = 2 * PEER_HEADS
        dh = peer_keys.shape[-1]
        q_slabs = matmul_slabs(h, w_peer_q[i].astype(BF16), tm=tm, tn=d)
        keys = peer_keys[i].reshape(n_hc, N_KEYS, dh).astype(BF16)
        sel = peer_select(q_slabs, keys)
        u_packed = pack_table(peer_u, i, transpose=False)
        vt_packed = pack_table(peer_v, i, transpose=True)
        h = peer_dense(h.T.astype(BF16), u_packed, vt_packed, sel, h,
                       ln_g[i, 1], ln_b[i, 1], alpha=alpha, tt=tm, te=PEER_EXPERT_TILE)

        last = i == depth - 1
        out = ple_add(h, p_prompt.reshape(depth, tp, -1), i, tail_tile(p_sample[i].reshape(bs, -1)),
                      w_ple_gate[i].astype(BF16), w_ple_proj[i].astype(BF16), tm=tm, split=last)
        if not last:
            h = out
    h_prompt, h_tail = out

    return (h_prompt.reshape(batch, seq, d), h_tail[:bs].reshape(bs, 1, d),
            jnp.stack(kp_l), jnp.stack(vp_l), jnp.stack(ks_l), jnp.stack(vs_l),
            jnp.stack(cp_l), jnp.stack(cs_l))
```

```python
import functools
import math

import jax
import jax.numpy as jnp
from jax import lax
from jax.experimental import pallas as pl
from jax.experimental.pallas import tpu as pltpu

BF16 = jnp.bfloat16
F32 = jnp.float32

ATT_HEADS = 8
ATT_HD = 64
ATT_W = 2 * ATT_HD
N_MIXERS = 2
CONV_W = 3
PEER_HEADS = 8
N_KEYS = 128
PEER_TOPK = 16
LN_EPS = 1e-5
SUBLN_EPS = 1e-5
NEG_INF = -1e30
LOG2E = math.log2(math.e)
LANES = 128
SUBLANES = 8
VMEM_LIMIT = 56 * 1024 * 1024


def _params(*sem, flags=None):
    return pltpu.CompilerParams(dimension_semantics=sem, vmem_limit_bytes=VMEM_LIMIT, flags=flags)


def _nt_dot(a, b):
    return lax.dot_general(a, b, (((1,), (1,)), ((), ())), preferred_element_type=F32)


def _layer_norm(z, g, b):
    mu = jnp.mean(z, axis=-1, keepdims=True)
    zc = z - mu
    var = jnp.mean(zc * zc, axis=-1, keepdims=True)
    return zc * lax.rsqrt(var + LN_EPS) * g + b


def _diff_lambda(wl, lam_init):
    a = jnp.sum(wl[0:1] * wl[1:2], axis=1, keepdims=True)
    b = jnp.sum(wl[2:3] * wl[3:4], axis=1, keepdims=True)
    return jnp.exp(a) - jnp.exp(b) + lam_init


def _mm_kernel(x_ref, w_ref, o_ref):
    o_ref[...] = jnp.dot(x_ref[...].astype(BF16), w_ref[...], preferred_element_type=F32)


def matmul_slabs(x, w, *, tm, tn):
    m, k = x.shape
    n = w.shape[1]
    return pl.pallas_call(
        _mm_kernel,
        grid=(m // tm, n // tn),
        in_specs=[pl.BlockSpec((tm, k), lambda i, j: (i, 0)),
                  pl.BlockSpec((k, tn), lambda i, j: (0, j))],
        out_specs=pl.BlockSpec((None, tm, tn), lambda i, j: (j, i, 0)),
        out_shape=jax.ShapeDtypeStruct((n // tn, m, tn), F32),
        compiler_params=_params("parallel", "parallel"),
        name="matmul_slabs",
    )(x, w)


def _qkv_heads_kernel(x_ref, w_ref, o_ref, kh_ref, vh_ref):
    j = pl.program_id(1)
    y = jnp.dot(x_ref[...].astype(BF16), w_ref[...], preferred_element_type=F32)
    o_ref[...] = y

    def heads_out(ref):
        for h in range(ATT_HEADS):
            ref[:, h, :] = y[:, h * ATT_W:(h + 1) * ATT_W]

    pl.when(j == 1)(functools.partial(heads_out, kh_ref))
    pl.when(j == 2)(functools.partial(heads_out, vh_ref))


def qkv_project(x, w, *, tm):
    m, k = x.shape
    width = ATT_HEADS * ATT_W
    heads = jax.ShapeDtypeStruct((m, ATT_HEADS, ATT_W), F32)
    heads_spec = pl.BlockSpec((tm, ATT_HEADS, ATT_W), lambda i, j: (i, 0, 0))
    return pl.pallas_call(
        _qkv_heads_kernel,
        grid=(m // tm, 3),
        in_specs=[pl.BlockSpec((tm, k), lambda i, j: (i, 0)),
                  pl.BlockSpec((k, width), lambda i, j: (0, j))],
        out_specs=[pl.BlockSpec((None, tm, width), lambda i, j: (j, i, 0)), heads_spec, heads_spec],
        out_shape=[jax.ShapeDtypeStruct((3, m, width), F32), heads, heads],
        compiler_params=_params("parallel", "arbitrary"),
        name="qkv_project",
    )(x, w)


FLASH_ROW_GROUPS = 8


def _flash_kernel(slope_ref, wl_ref, g_ref, q_ref, k_ref, v_ref, o_ref,
                  q_s, m_s, l_s, a_s, *, tq, tk, lam_init):
    qi = pl.program_id(2)
    ki = pl.program_id(3)
    nk = pl.num_programs(3)
    q0 = qi * tq
    k0 = ki * tk

    @pl.when(ki == 0)
    def _init():
        q = q_ref[...] * (ATT_HD ** -0.5 * LOG2E)
        lane = lax.broadcasted_iota(jnp.int32, q.shape, 1)
        q_s[0:tq] = jnp.where(lane < ATT_HD, q, 0.0).astype(BF16)
        q_s[tq:2 * tq] = jnp.where(lane >= ATT_HD, q, 0.0).astype(BF16)
        m_s[...] = jnp.full(m_s.shape, NEG_INF, F32)
        l_s[...] = jnp.zeros(l_s.shape, F32)
        a_s[...] = jnp.zeros(a_s.shape, F32)

    def step(on_diagonal):
        kb = k_ref[...].astype(BF16)
        vb = jnp.concatenate([v_ref[...].astype(BF16), jnp.ones((tk, LANES), BF16)], axis=1)
        col = lax.broadcasted_iota(jnp.int32, (1, tk), 1)
        bias = (slope_ref[...] * LOG2E) * (k0 + col - q0).astype(F32)
        rg = 2 * tq // FLASH_ROW_GROUPS
        groups = [slice(g * rg, (g + 1) * rg) for g in range(FLASH_ROW_GROUPS)]
        scores = [_nt_dot(q_s[rows, :], kb) for rows in groups]
        for g, rows in enumerate(groups):
            s = scores[g] + bias
            if on_diagonal:
                r = lax.broadcasted_iota(jnp.int32, (rg, tk), 0) + (g * rg) % tq
                c = lax.broadcasted_iota(jnp.int32, (rg, tk), 1)
                s = jnp.where(c > r, NEG_INF, s)
            m_old = m_s[rows, :]
            m_new = jnp.maximum(m_old, jnp.max(s, axis=1, keepdims=True))
            alpha = jnp.exp2(m_old - m_new)
            p = jnp.exp2(s - jnp.tile(m_new, (1, tk // LANES)))
            pv = jnp.dot(p.astype(BF16), vb, preferred_element_type=F32)
            l_s[rows, :] = alpha * l_s[rows, :] + pv[:, ATT_W:]
            a_s[rows, :] = alpha * a_s[rows, :] + pv[:, :ATT_W]
            m_s[rows, :] = m_new

    pl.when(ki < qi)(functools.partial(step, False))
    pl.when(ki == qi)(functools.partial(step, True))

    @pl.when(ki == nk - 1)
    def _finish():
        lam = _diff_lambda(wl_ref[...], lam_init)
        w = a_s[...] / l_s[...]
        o = w[0:tq] - lam * w[tq:2 * tq]
        ms = jnp.mean(o * o, axis=1, keepdims=True)
        o_ref[...] = o * lax.rsqrt(ms + SUBLN_EPS) * g_ref[...] * (1.0 - lam_init)


def flash_prompt(qkv, slopes, w_lam, subln_g, *, batch, seq, tq, tk, lam_init):
    assert tq == tk
    nq = seq // tq
    nkb = seq // tk

    def kv_row(b, qi, ki):
        return b * nkb + jnp.minimum(ki, qi)

    kern = functools.partial(_flash_kernel, tq=tq, tk=tk, lam_init=lam_init)
    return pl.pallas_call(
        kern,
        grid=(batch, ATT_HEADS, nq, nkb),
        in_specs=[
            pl.BlockSpec((None, 1, tk), lambda b, h, qi, ki: (h, 0, 0)),
            pl.BlockSpec((4, ATT_HD), lambda b, h, qi, ki: (0, 0)),
            pl.BlockSpec((1, ATT_W), lambda b, h, qi, ki: (0, 0)),
            pl.BlockSpec((None, tq, ATT_W), lambda b, h, qi, ki: (0, b * nq + qi, h)),
            pl.BlockSpec((None, tk, ATT_W), lambda b, h, qi, ki: (1, kv_row(b, qi, ki), h)),
            pl.BlockSpec((None, tk, ATT_W), lambda b, h, qi, ki: (2, kv_row(b, qi, ki), h)),
        ],
        out_specs=pl.BlockSpec((tq, ATT_W), lambda b, h, qi, ki: (b * nq + qi, h)),
        out_shape=jax.ShapeDtypeStruct((batch * seq, ATT_HEADS * ATT_W), F32),
        scratch_shapes=[
            pltpu.VMEM((2 * tq, ATT_W), BF16), pltpu.VMEM((2 * tq, LANES), F32),
            pltpu.VMEM((2 * tq, LANES), F32), pltpu.VMEM((2 * tq, ATT_W), F32),
        ],
        compiler_params=_params("parallel", "parallel", "parallel", "arbitrary"),
        name="flash_prompt",
    )(slopes, w_lam, subln_g, qkv, qkv, qkv)


DECODE_PAGES_PER_STEP = 4


def _decode_kernel(pt_ref, q_ref, kn_ref, vn_ref, *rest, n_grp, page, past, lam_init):
    del pt_ref
    k_refs = rest[:n_grp]
    v_refs = rest[n_grp:2 * n_grp]
    bsel_ref, slope_ref, alibi_ref, g_ref, wl_ref, o_ref, m_s, l_s, aa_s, ab_s = rest[2 * n_grp:]
    pg = pl.program_id(1)
    n_steps = pl.num_programs(1)
    nh = ATT_HEADS
    q8 = q_ref[...] * (ATT_HD ** -0.5 * LOG2E)

    def half_sums(prod):
        return jnp.dot(prod.astype(BF16), bsel_ref[...], preferred_element_type=F32)

    def swap_halves(x):
        return pltpu.roll(x, ATT_HD, x.ndim - 1)

    @pl.when(pg == 0)
    def _init():
        m_s[...] = jnp.full(m_s.shape, NEG_INF, F32)
        for ref in (l_s, aa_s, ab_s):
            ref[...] = jnp.zeros(ref.shape, F32)

    slope = slope_ref[...]
    logits, shifts = [], []
    m_new = m_s[...]
    for g in range(n_grp):
        prod = (k_refs[g][...] * q8[None]).reshape(page * nh, ATT_W)
        s3 = half_sums(prod).reshape(page, nh, ATT_W) + alibi_ref[...]
        shift = slope * (past - (pg * n_grp + g) * page).astype(F32)
        m_new = jnp.maximum(m_new, jnp.max(s3, axis=0) - shift)
        logits.append(s3)
        shifts.append(shift)
    alpha = jnp.exp2(m_s[...] - m_new)
    l = alpha * l_s[...]
    acc_a = alpha * aa_s[...]
    acc_b = swap_halves(alpha) * ab_s[...]
    for g in range(n_grp):
        pe = jnp.exp2(logits[g] - (m_new + shifts[g])[None])
        v3 = v_refs[g][...]
        l = l + jnp.sum(pe, axis=0)
        acc_a = acc_a + jnp.sum(pe * v3, axis=0)
        acc_b = acc_b + jnp.sum(swap_halves(pe) * v3, axis=0)
    m_s[...] = m_new
    l_s[...] = l
    aa_s[...] = acc_a
    ab_s[...] = acc_b

    @pl.when(pg == n_steps - 1)
    def _finish():
        s_self = half_sums(q8 * kn_ref[...])
        m_n = jnp.maximum(m_new, s_self)
        al = jnp.exp2(m_new - m_n)
        p_self = jnp.exp2(s_self - m_n)
        lf = al * l + p_self
        vn = vn_ref[...]
        fa = al * acc_a + p_self * vn
        fb = swap_halves(al) * acc_b + swap_halves(p_self) * vn
        first = lax.broadcasted_iota(jnp.int32, fa.shape, 1) < ATT_HD
        lf_sw = swap_halves(lf)
        o1 = jnp.where(first, fa, fb) / jnp.where(first, lf, lf_sw)
        o2 = jnp.where(first, fb, fa) / jnp.where(first, lf_sw, lf)
        lam = _diff_lambda(wl_ref[...], lam_init)
        d = o1 - lam * o2
        ms = jnp.mean(d * d, axis=1, keepdims=True)
        o_ref[...] = d * lax.rsqrt(ms + SUBLN_EPS) * g_ref[...] * (1.0 - lam_init)


def decode_sample(page_table, q, k_new, v_new, cache_k, cache_v, w_lam, subln_g, *, layer, lam_init):
    bs, n_pages = page_table.shape
    page = cache_k.shape[2]
    nh = ATT_HEADS
    past = n_pages * page
    n_grp = math.gcd(n_pages, DECODE_PAGES_PER_STEP)
    half = jnp.arange(ATT_W)[:, None] // ATT_HD == jnp.arange(ATT_W)[None, :] // ATT_HD
    bsel = half.astype(BF16)
    slopes = jnp.exp2(-8.0 * jnp.arange(1, nh + 1, dtype=F32) / nh) * LOG2E
    slope = jnp.broadcast_to(slopes[:, None], (nh, ATT_W))
    alibi = jnp.arange(page, dtype=F32)[:, None, None] * slope[None]

    row_spec = pl.BlockSpec((None, nh, ATT_W), lambda b, p, pt: (b, 0, 0))
    const = lambda shape: pl.BlockSpec(shape, lambda b, p, pt: (0,) * len(shape))

    def page_spec(g):
        return pl.BlockSpec((None, None, page, nh, ATT_W),
                            lambda b, p, pt: (layer, pt[b, p * n_grp + g], 0, 0, 0))

    pages = [page_spec(g) for g in range(n_grp)]
    kern = functools.partial(_decode_kernel, n_grp=n_grp, page=page, past=past, lam_init=lam_init)
    return pl.pallas_call(
        kern,
        grid_spec=pltpu.PrefetchScalarGridSpec(
            num_scalar_prefetch=1,
            grid=(bs, n_pages // n_grp),
            in_specs=[row_spec, row_spec, row_spec] + pages + pages + [
                const((ATT_W, ATT_W)), const((nh, ATT_W)), const((page, nh, ATT_W)),
                const((1, ATT_W)), const((4, ATT_HD))],
            out_specs=pl.BlockSpec((None, nh, ATT_W), lambda b, p, pt: (b, 0, 0)),
            scratch_shapes=[pltpu.VMEM((nh, ATT_W), F32)] * 4,
        ),
        out_shape=jax.ShapeDtypeStruct((bs, nh, ATT_W), F32),
        compiler_params=_params("parallel", "arbitrary"),
        name="decode_sample",
    )(page_table, q, k_new, v_new, *([cache_k] * n_grp), *([cache_v] * n_grp),
      bsel, slope, alibi, subln_g[None, :], w_lam)


def _mm_res_ln_kernel(xp_ref, xt_ref, w_ref, rp_ref, rt_ref, g_ref, b_ref, o_ref, *, alpha, n_prompt_tiles):
    tail = pl.program_id(0) >= n_prompt_tiles
    x = jnp.where(tail, xt_ref[...], xp_ref[...])
    res = jnp.where(tail, rt_ref[...], rp_ref[...])
    y = jnp.dot(x.astype(BF16), w_ref[...], preferred_element_type=F32)
    o_ref[...] = _layer_norm(alpha * res + y, g_ref[...], b_ref[...])


def mm_res_ln(x_prompt, x_tail, w, res_prompt, res_tail, g, b, *, alpha, tm):
    tp, k = x_prompt.shape
    n = w.shape[1]
    n_p = tp // tm
    prompt = lambda width: pl.BlockSpec((tm, width), lambda i: (jnp.minimum(i, n_p - 1), 0))
    full = lambda shape: pl.BlockSpec(shape, lambda i: (0, 0))
    return pl.pallas_call(
        functools.partial(_mm_res_ln_kernel, alpha=alpha, n_prompt_tiles=n_p),
        grid=(n_p + 1,),
        in_specs=[prompt(k), full((tm, k)), full((k, n)), prompt(n), full((tm, n)),
                  full((1, n)), full((1, n))],
        out_specs=pl.BlockSpec((tm, n), lambda i: (i, 0)),
        out_shape=jax.ShapeDtypeStruct((tp + tm, n), F32),
        compiler_params=_params("parallel"),
        name="mm_res_ln",
    )(x_prompt, x_tail, w, res_prompt, res_tail, g[None, :], b[None, :])


CARRY_ROWS = 8


def _conv_kernel(*refs, alpha, tm, chained):
    if chained:
        x_ref, win_ref, cw_ref, wout_ref, g_ref, b_ref, o_ref, u_ref, carry_s = refs
    else:
        x_ref, l0_ref, l1_ref, win_ref, cw_ref, wout_ref, g_ref, b_ref, joint_ref, o_ref, u_ref = refs
        del joint_ref
    d = x_ref.shape[1]
    x = x_ref[...]
    bch = jnp.dot(x.astype(BF16), win_ref[...], preferred_element_type=F32)
    b_g = bch[:, 0:d]
    u = bch[:, d:2 * d] * bch[:, 2 * d:3 * d]
    cw = cw_ref[...]
    if chained:
        i = pl.program_id(1)

        @pl.when(i == 0)
        def _zero_left():
            carry_s[...] = jnp.zeros(carry_s.shape, F32)

        prev = carry_s[...]
        row = lax.broadcasted_iota(jnp.int32, u.shape, 0)
        last = prev[CARRY_ROWS - 1:CARRY_ROWS]
        u1 = jnp.where(row == 0, last, pltpu.roll(u, 1, 0))
        u2 = jnp.where(row == 0, prev[CARRY_ROWS - 2:CARRY_ROWS - 1],
                       jnp.where(row == 1, last, pltpu.roll(u, 2, 0)))
        tail = u[tm - CARRY_ROWS:tm]
        carry_s[...] = tail
        u_ref[...] = tail
    else:
        u2 = l0_ref[...]
        u1 = l1_ref[...]
        u_ref[...] = u
    z = cw[0:1] * u2 + cw[1:2] * u1 + cw[2:3] * u
    y = jnp.dot((b_g * z).astype(BF16), wout_ref[...], preferred_element_type=F32)
    o_ref[...] = _layer_norm(alpha * x + y, g_ref[...], b_ref[...])


def conv_prompt(x, w_in, conv_w, w_out, g, b, *, batch, seq, alpha, tm):
    t_rows, d = x.shape
    nt = seq // tm
    rows = pl.BlockSpec((tm, d), lambda bi, i: (bi * nt + i, 0))
    full = lambda shape: pl.BlockSpec(shape, lambda bi, i: (0, 0))
    return pl.pallas_call(
        functools.partial(_conv_kernel, alpha=alpha, tm=tm, chained=True),
        grid=(batch, nt),
        in_specs=[rows, full((d, 3 * d)), full((CONV_W, d)), full((d, d)), full((1, d)), full((1, d))],
        out_specs=[rows, pl.BlockSpec((None, CARRY_ROWS, d), lambda bi, i: (bi, 0, 0))],
        out_shape=[jax.ShapeDtypeStruct((t_rows, d), F32),
                   jax.ShapeDtypeStruct((batch, CARRY_ROWS, d), F32)],
        scratch_shapes=[pltpu.VMEM((CARRY_ROWS, d), F32)],
        compiler_params=_params("parallel", "arbitrary"),
        name="conv_prompt",
    )(x, w_in, conv_w, w_out, g[None, :], b[None, :])


def conv_sample(x, joint_out, left0, left1, w_in, conv_w, w_out, g, b, *, alpha, tm):
    t_rows, d = x.shape
    last = t_rows // tm - 1
    tile = pl.BlockSpec((tm, d), lambda i: (last, 0))
    full = lambda shape: pl.BlockSpec(shape, lambda i: (0, 0))
    return pl.pallas_call(
        functools.partial(_conv_kernel, alpha=alpha, tm=tm, chained=False),
        grid=(1,),
        in_specs=[tile, full((tm, d)), full((tm, d)), full((d, 3 * d)), full((CONV_W, d)),
                  full((d, d)), full((1, d)), full((1, d)), pl.BlockSpec(memory_space=pl.ANY)],
        out_specs=[tile, full((tm, d))],
        out_shape=[jax.ShapeDtypeStruct((t_rows, d), F32), jax.ShapeDtypeStruct((tm, d), F32)],
        input_output_aliases={8: 0},
        compiler_params=_params("arbitrary"),
        name="conv_sample",
    )(x, left0, left1, w_in, conv_w, w_out, g[None, :], b[None, :], joint_out)


PACK = 4 // jnp.dtype(BF16).itemsize


def _pack_rows(x):
    return pltpu.bitcast(x.astype(BF16), jnp.uint32)


def _unpack_rows(x):
    return pltpu.bitcast(x, BF16)


def _replicate_word(x):
    if PACK == 1:
        return pltpu.bitcast(x, jnp.uint32)
    hi = pltpu.bitcast(x.astype(BF16).astype(F32), jnp.uint32)
    return hi | (hi >> 16)


def _top_values(x, count, store, want_rank=False):
    rank = jnp.full(x.shape, float(count), F32) if want_rank else None
    for r in range(count):
        mx = jnp.max(x, axis=0, keepdims=True)
        store(r, mx)
        hit = x == mx
        if want_rank:
            rank = jnp.where(hit, float(r), rank)
        if r + 1 < count:
            x = jnp.where(hit, -jnp.inf, x)
    return rank


def _select_kernel(q_ref, keys_ref, cnt_ref, e1_ref, r2_ref, e2_ref, s1_s, sv_s):
    k = PEER_TOPK
    dh = keys_ref.shape[2]
    per_slab = q_ref.shape[2] // dh
    for hc in range(2 * PEER_HEADS):
        h, second = divmod(hc, 2)
        qb = q_ref[hc // per_slab, :, (hc % per_slab) * dh:(hc % per_slab + 1) * dh].astype(BF16)
        s = _nt_dot(keys_ref[hc], qb)

        def store(r, mx, hc=hc):
            sv_s[hc, r:r + 1, :] = mx

        rank = _top_values(s, k, store, want_rank=bool(second))
        if second:
            r2_ref[0, h] = _pack_rows(rank)
            e2_ref[0, h] = _pack_rows(jnp.exp(s - sv_s[hc, 0:1, :]))
        else:
            s1_s[h] = s

    for h in range(PEER_HEADS):
        sv1 = sv_s[2 * h]
        sv2 = sv_s[2 * h + 1]
        sub = lax.broadcasted_iota(jnp.int32, (SUBLANES, LANES), 0)
        pieces = [sv1[0:1] + sv2]
        for a in range(2, SUBLANES + 1):
            sums = sv1[a - 1:a] + sv2[0:SUBLANES]
            pieces.append(sums if k // a >= SUBLANES else jnp.where(sub < k // a, sums, -jnp.inf))
        pieces.append(sv1[SUBLANES:k] + sv2[0:1])
        cand = jnp.concatenate(pieces, axis=0)
        tau_box = []
        _top_values(cand, k, lambda r, mx: tau_box.append(mx))
        tau = tau_box[-1]
        top = sv1[0:1] + sv2[0:1]
        z = jnp.sum(jnp.where(cand >= tau, jnp.exp(cand - top), 0.0), axis=0, keepdims=True)
        s1 = s1_s[h]
        cnt = jnp.zeros(s1.shape, F32)
        for b in range(k // 2):
            cnt = cnt + jnp.where(s1 + sv2[b:b + 1] >= tau, 1.0, 0.0)
        cnt_best = jnp.zeros((1, LANES), F32)
        for b in range(k // 2, k):
            cnt_best = cnt_best + jnp.where(sv1[0:1] + sv2[b:b + 1] >= tau, 1.0, 0.0)
        cnt_ref[0, h] = _replicate_word(jnp.where(s1 == sv1[0:1], cnt + cnt_best, cnt))
        e1_ref[0, h] = _replicate_word(jnp.exp(s1 - sv1[0:1]) * (0.5 / z))


def peer_select(q_slabs, keys):
    n_slab, t, slab_w = q_slabs.shape
    nhc, _, dh = keys.shape
    nchunk = t // LANES
    def out(rows, dtype):
        spec = pl.BlockSpec((1, PEER_HEADS, rows, LANES), lambda i: (i, 0, 0, 0))
        return spec, jax.ShapeDtypeStruct((nchunk, PEER_HEADS, rows, LANES), dtype)

    outs = [out(N_KEYS, jnp.uint32), out(N_KEYS, jnp.uint32),
            out(N_KEYS // PACK, jnp.uint32), out(N_KEYS // PACK, jnp.uint32)]
    return pl.pallas_call(
        _select_kernel,
        grid=(nchunk,),
        in_specs=[pl.BlockSpec((n_slab, LANES, slab_w), lambda i: (0, i, 0)),
                  pl.BlockSpec((nhc, N_KEYS, dh), lambda i: (0, 0, 0))],
        out_specs=[spec for spec, _ in outs],
        out_shape=[sds for _, sds in outs],
        scratch_shapes=[pltpu.VMEM((PEER_HEADS, N_KEYS, LANES), F32),
                        pltpu.VMEM((nhc, PEER_TOPK, LANES), F32)],
        compiler_params=_params("parallel"),
        name="peer_select",
    )(q_slabs, keys)


GATE_ROWS = 16
MXU_PIECES = 4
REGIONS_PER_HALF = 1


def _gelu_x2(a):
    return a * (1.0 + lax.erf(a * (2.0 ** -0.5)))


def _peer_dense_kernel(xt_ref, u_ref, vt_ref, cnt_ref, e1_ref, r2_ref, e2_ref, res_ref,
                       g_ref, b_ref, o_ref, a0_s, a1_s, w0_s, w1_s, acc_s, *, alpha, te, tt, n_tiles):
    gstep = pl.program_id(1)
    n_steps = pl.num_programs(1)
    n_i = te // N_KEYS

    @pl.when(gstep == 0)
    def _init():
        for ref in (a0_s, a1_s, w0_s, w1_s, acc_s):
            ref[...] = jnp.zeros(ref.shape, ref.dtype)

    assert n_i == 4 and tt % (2 * LANES) == 0
    th = tt // 2

    def gate_block(iis, tc, rbs, i0, a_ref, w_ref):
        lanes = slice(tc * LANES, (tc + 1) * LANES)

        def row(ref, ii, h):
            r = ref[tc, h, pl.ds(i0 + ii, 1), :]
            return _unpack_rows(jnp.broadcast_to(r, (GATE_ROWS // PACK, LANES)))

        cnt = {(ii, h): row(cnt_ref, ii, h) for ii in iis for h in range(PEER_HEADS)}
        e1 = {(ii, h): row(e1_ref, ii, h) for ii in iis for h in range(PEER_HEADS)}
        zero = jnp.zeros((GATE_ROWS, LANES), BF16)
        for rb in rbs:
            keys = slice(rb * GATE_ROWS // PACK, (rb + 1) * GATE_ROWS // PACK)
            gate = {ii: zero for ii in iis}
            for h in range(PEER_HEADS):
                r2 = _unpack_rows(r2_ref[tc, h, keys, :])
                e2 = _unpack_rows(e2_ref[tc, h, keys, :])
                for ii in iis:
                    gate[ii] = gate[ii] + jnp.where(r2 < cnt[ii, h], e2, zero) * e1[ii, h]
            for ii in iis:
                out_rows = slice(ii * N_KEYS + rb * GATE_ROWS, ii * N_KEYS + (rb + 1) * GATE_ROWS)
                w_ref[out_rows, lanes] = gate[ii] * _gelu_x2(a_ref[out_rows, lanes].astype(BF16))

    def half_step(tile, a_src, w_dst, w_src, vt0, a_dst, u0, region_base):
        i0 = jnp.clip(tile, 0, n_tiles - 1) * n_i
        n_tc = tt // LANES
        def quarter(r):
            iis = (2 * (r // 2), 2 * (r // 2) + 1)
            n_rb = N_KEYS // GATE_ROWS
            units = [(tc, range(part * n_rb // 2, (part + 1) * n_rb // 2))
                     for tc in range((r % 2) * n_tc // 2, (r % 2 + 1) * n_tc // 2) for part in range(2)]
            for piece in range(MXU_PIECES):
                if r < 2:
                    tok = slice(r * th, (r + 1) * th)
                    rows = slice(piece * (d // MXU_PIECES), (piece + 1) * (d // MXU_PIECES))
                    vt = _unpack_rows(vt_ref[rows.start // PACK:rows.stop // PACK, vt0:vt0 + te])
                    acc_s[rows, tok] += jnp.dot(vt, w_src[:, tok], preferred_element_type=F32)
                else:
                    tok = slice((r - 2) * th, (r - 1) * th)
                    rows = slice(piece * (te // MXU_PIECES), (piece + 1) * (te // MXU_PIECES))
                    u = _unpack_rows(u_ref[(u0 + rows.start) // PACK:(u0 + rows.stop) // PACK, :])
                    a_dst[rows, tok] = jnp.dot(u, xt_ref[:, tok], preferred_element_type=F32)
                for tc, rbs in units[piece * len(units) // MXU_PIECES:(piece + 1) * len(units) // MXU_PIECES]:
                    gate_block(iis, tc, rbs, i0, a_src, w_dst)

        per_region = n_i // REGIONS_PER_HALF
        for region in range(REGIONS_PER_HALF):
            @pl.when(gstep < n_steps + region_base + region)
            def _region(region=region):
                for r in range(region * per_region, (region + 1) * per_region):
                    quarter(r)

    d = acc_s.shape[0]
    half_step(2 * gstep - 1, a1_s, w1_s, w0_s, 0, a0_s, 0, 0)
    half_step(2 * gstep, a0_s, w0_s, w1_s, te, a1_s, te, REGIONS_PER_HALF)

    @pl.when(gstep == n_steps - 1)
    def _finish():
        y = acc_s[...].T
        o_ref[...] = _layer_norm(alpha * res_ref[...] + y, g_ref[...], b_ref[...])


def _pack_table_kernel(x_ref, o_ref, *, transpose):
    x = x_ref[...]
    o_ref[...] = _pack_rows(x.T if transpose else x)


def pack_table(tables, layer, *, transpose, tile=1024):
    _, rows, cols = tables.shape
    if transpose:
        out_shape = (cols // PACK, rows)
        out_spec = pl.BlockSpec((cols // PACK, tile), lambda i: (0, i))
    else:
        out_shape = (rows // PACK, cols)
        out_spec = pl.BlockSpec((tile // PACK, cols), lambda i: (i, 0))
    return pl.pallas_call(
        functools.partial(_pack_table_kernel, transpose=transpose),
        grid=(rows // tile,),
        in_specs=[pl.BlockSpec((None, tile, cols), lambda i: (layer, i, 0))],
        out_specs=out_spec,
        out_shape=jax.ShapeDtypeStruct(out_shape, jnp.uint32),
        compiler_params=_params("parallel"),
        name="pack_table",
    )(tables)


def peer_dense(xt, u, vt, sel, res, g, b, *, alpha, tt, te):
    d, t = xt.shape
    n_exp = u.shape[0] * PACK
    n_tiles = n_exp // te
    assert n_tiles % 2 == 0
    n_pairs = n_tiles // 2
    nchunk = tt // LANES
    sel_specs = [pl.BlockSpec((nchunk,) + a.shape[1:], lambda ti, s: (ti, 0, 0, 0)) for a in sel]
    full = lambda shape: pl.BlockSpec(shape, lambda ti, s: (0, 0))
    u_spec = pl.BlockSpec((2 * te // PACK, d), lambda ti, s: (jnp.minimum(s, n_pairs - 1), 0))
    vt_spec = pl.BlockSpec((d // PACK, 2 * te), lambda ti, s: (0, jnp.maximum(s - 1, 0)))
    return pl.pallas_call(
        functools.partial(_peer_dense_kernel, alpha=alpha, te=te, tt=tt, n_tiles=n_tiles),
        grid=(t // tt, n_pairs + 1),
        in_specs=[pl.BlockSpec((d, tt), lambda ti, s: (0, ti)), u_spec, vt_spec,
                  *sel_specs,
                  pl.BlockSpec((tt, d), lambda ti, s: (ti, 0)),
                  full((1, d)), full((1, d))],
        out_specs=pl.BlockSpec((tt, d), lambda ti, s: (ti, 0)),
        out_shape=jax.ShapeDtypeStruct((t, d), F32),
        scratch_shapes=[pltpu.VMEM((te, tt), F32), pltpu.VMEM((te, tt), F32),
                        pltpu.VMEM((te, tt), BF16), pltpu.VMEM((te, tt), BF16),
                        pltpu.VMEM((d, tt), F32)],
        compiler_params=_params("parallel", "arbitrary"),
        name="peer_dense",
    )(xt, u, vt, *sel, res, g[None, :], b[None, :])


def _ple_kernel(h_ref, pp_ref, pt_ref, wg_ref, wp_ref, *o_refs, n_prompt_tiles):
    i = pl.program_id(0)
    tail = i >= n_prompt_tiles
    h = h_ref[...]
    p = jnp.where(tail, pt_ref[...], pp_ref[...])
    gate = jax.nn.sigmoid(jnp.dot(h.astype(BF16), wg_ref[...], preferred_element_type=F32))
    proj = jnp.dot(p.astype(BF16), wp_ref[...], preferred_element_type=F32)
    out = h + gate * proj
    if len(o_refs) == 1:
        o_refs[0][...] = out
    else:
        op_ref, ot_ref = o_refs

        @pl.when(jnp.logical_not(tail))
        def _():
            op_ref[...] = out

        @pl.when(tail)
        def _():
            ot_ref[...] = out


def ple_add(h, p_prompt, layer, p_tail, wg, wp, *, tm, split):
    m, d = h.shape
    pd = p_tail.shape[1]
    n_p = m // tm - 1
    prompt_tile = lambda i: jnp.minimum(i, n_p - 1)
    full = lambda shape: pl.BlockSpec(shape, lambda i: (0, 0))
    if split:
        out_specs = [pl.BlockSpec((tm, d), lambda i: (prompt_tile(i), 0)), full((tm, d))]
        out_shape = [jax.ShapeDtypeStruct((n_p * tm, d), F32), jax.ShapeDtypeStruct((tm, d), F32)]
    else:
        out_specs = pl.BlockSpec((tm, d), lambda i: (i, 0))
        out_shape = jax.ShapeDtypeStruct((m, d), F32)
    return pl.pallas_call(
        functools.partial(_ple_kernel, n_prompt_tiles=n_p),
        grid=(m // tm,),
        in_specs=[pl.BlockSpec((tm, d), lambda i: (i, 0)),
                  pl.BlockSpec((None, tm, pd), lambda i: (layer, prompt_tile(i), 0)),
                  full((tm, pd)), full((d, d)), full((pd, d))],
        out_specs=out_specs,
        out_shape=out_shape,
        compiler_params=_params("arbitrary"),
        name="ple_add",
    )(h, p_prompt, p_tail, wg, wp)


TOKEN_TILE = 512
PEER_EXPERT_TILE = 512


def _largest_tile(n, candidates):
    for c in candidates:
        if n % c == 0:
            return c
    raise ValueError(f"no tile in {candidates} divides {n}")


def kernel(x_prompt, x_sample, cache_k, cache_v, state_conv, page_table, p_prompt, p_sample,
           ln_g, ln_b, w_attn_qkv, w_attn_lambda, attn_subln_g, w_attn_o,
           w_conv_in, conv_w, w_conv_out, w_peer_q, peer_keys, peer_u, peer_v,
           w_ple_gate, w_ple_proj):
    batch, seq, d = x_prompt.shape
    bs = x_sample.shape[0]
    assert x_sample.shape[1] == 1
    depth = ln_g.shape[0]
    tp = batch * seq
    t = tp + bs
    tm = TOKEN_TILE
    t_pad = -(-t // tm) * tm
    alpha = (2 * depth) ** 0.25
    width = ATT_HEADS * ATT_W
    tseq = _largest_tile(seq, (512, 256, 128))
    tflash = _largest_tile(seq, (1024, 512, 256, 128))

    assert tp % tm == 0 and t_pad == tp + tm

    def tail_tile(sample_rows):
        pad = jnp.zeros((tm - bs, sample_rows.shape[1]), F32)
        return jnp.concatenate([sample_rows, pad], axis=0)

    h = None
    slopes = jnp.exp2(-8.0 * jnp.arange(1, ATT_HEADS + 1, dtype=F32) / ATT_HEADS)
    slopes_b = jnp.broadcast_to(slopes[:, None, None], (ATT_HEADS, 1, tflash))

    kp_l, vp_l, ks_l, vs_l, cp_l, cs_l = [], [], [], [], [], []
    for i in range(depth):
        j = i // N_MIXERS
        if i % N_MIXERS == 0:
            lam_init = 0.8 - 0.6 * math.exp(-0.3 * i)
            w_qkv = w_attn_qkv[j].astype(BF16)
            rows_p = x_prompt.reshape(tp, d) if h is None else h[:tp]
            rows_s = x_sample.reshape(bs, d) if h is None else h[tp:t]
            qkv, k_heads, v_heads = qkv_project(rows_p, w_qkv, tm=tseq)
            qkv_s = matmul_slabs(rows_s, w_qkv, tm=bs, tn=width).reshape(3, bs, ATT_HEADS, ATT_W)
            g_sub = attn_subln_g[j]
            o_p = flash_prompt(qkv, slopes_b, w_attn_lambda[j], g_sub[None, :], batch=batch, seq=seq,
                               tq=tflash, tk=tflash, lam_init=lam_init)
            o_s = decode_sample(page_table, qkv_s[0], qkv_s[1], qkv_s[2], cache_k, cache_v,
                                w_attn_lambda[j], g_sub, layer=j, lam_init=lam_init)
            h = mm_res_ln(o_p, tail_tile(o_s.reshape(bs, width)), w_attn_o[j].astype(BF16),
                          rows_p, tail_tile(rows_s), ln_g[i, 0], ln_b[i, 0], alpha=alpha, tm=tm)
            kp_l.append(k_heads.reshape(batch, seq, ATT_HEADS, ATT_W))
            vp_l.append(v_heads.reshape(batch, seq, ATT_HEADS, ATT_W))
            ks_l.append(qkv_s[1].reshape(bs, 1, ATT_HEADS, ATT_W))
            vs_l.append(qkv_s[2].reshape(bs, 1, ATT_HEADS, ATT_W))
        else:
            if h is None:
                h = jnp.concatenate([x_prompt.reshape(tp, d), tail_tile(x_sample.reshape(bs, d))], axis=0)
            w_in = w_conv_in[j].astype(BF16)
            w_out = w_conv_out[j].astype(BF16)
            h_new, tail = conv_prompt(h, w_in, conv_w[j], w_out, ln_g[i, 0], ln_b[i, 0],
                                      batch=batch, seq=seq, alpha=alpha, tm=tseq)
            left = state_conv[j]
            h, u_tile = conv_sample(h, h_new, tail_tile(left[:, 0]), tail_tile(left[:, 1]), w_in,
                                    conv_w[j], w_out, ln_g[i, 0], ln_b[i, 0], alpha=alpha, tm=tm)
            cp_l.append(tail[:, CARRY_ROWS - (CONV_W - 1):])
            cs_l.append(jnp.stack([left[:, 1], u_tile[:bs]], axis=1))

        n_hc = 2 * PEER_HEADS
        dh = peer_keys.shape[-1]
        q_slabs = matmul_slabs(h, w_peer_q[i].astype(BF16), tm=tm, tn=d)
        keys = peer_keys[i].reshape(n_hc, N_KEYS, dh).astype(BF16)
        sel = peer_select(q_slabs, keys)
        u_packed = pack_table(peer_u, i, transpose=False)
        vt_packed = pack_table(peer_v, i, transpose=True)
        h = peer_dense(h.T.astype(BF16), u_packed, vt_packed, sel, h,
                       ln_g[i, 1], ln_b[i, 1], alpha=alpha, tt=tm, te=PEER_EXPERT_TILE)

        last = i == depth - 1
        out = ple_add(h, p_prompt.reshape(depth, tp, -1), i, tail_tile(p_sample[i].reshape(bs, -1)),
                      w_ple_gate[i].astype(BF16), w_ple_proj[i].astype(BF16), tm=tm, split=last)
        if not last:
            h = out
    h_prompt, h_tail = out

    return (h_prompt.reshape(batch, seq, d), h_tail[:bs].reshape(bs, 1, d),
            jnp.stack(kp_l), jnp.stack(vp_l), jnp.stack(ks_l), jnp.stack(vs_l),
            jnp.stack(cp_l), jnp.stack(cs_l))
```

```python
import functools
import math

import jax
import jax.numpy as jnp
from jax import lax
from jax.experimental import pallas as pl
from jax.experimental.pallas import tpu as pltpu

BF16 = jnp.bfloat16
F32 = jnp.float32

ATT_HEADS = 8
ATT_HD = 64
ATT_W = 2 * ATT_HD
N_MIXERS = 2
CONV_W = 3
PEER_HEADS = 8
N_KEYS = 128
PEER_TOPK = 16
LN_EPS = 1e-5
SUBLN_EPS = 1e-5
NEG_INF = -1e30
LOG2E = math.log2(math.e)
LANES = 128
SUBLANES = 8
VMEM_LIMIT = 56 * 1024 * 1024


def _params(*sem, flags=None):
    return pltpu.CompilerParams(dimension_semantics=sem, vmem_limit_bytes=VMEM_LIMIT, flags=flags)


def _nt_dot(a, b):
    return lax.dot_general(a, b, (((1,), (1,)), ((), ())), preferred_element_type=F32)


def _layer_norm(z, g, b):
    mu = jnp.mean(z, axis=-1, keepdims=True)
    zc = z - mu
    var = jnp.mean(zc * zc, axis=-1, keepdims=True)
    return zc * lax.rsqrt(var + LN_EPS) * g + b


def _diff_lambda(wl, lam_init):
    a = jnp.sum(wl[0:1] * wl[1:2], axis=1, keepdims=True)
    b = jnp.sum(wl[2:3] * wl[3:4], axis=1, keepdims=True)
    return jnp.exp(a) - jnp.exp(b) + lam_init


def _mm_kernel(x_ref, w_ref, o_ref):
    o_ref[...] = jnp.dot(x_ref[...].astype(BF16), w_ref[...], preferred_element_type=F32)


def matmul_slabs(x, w, *, tm, tn):
    m, k = x.shape
    n = w.shape[1]
    return pl.pallas_call(
        _mm_kernel,
        grid=(m // tm, n // tn),
        in_specs=[pl.BlockSpec((tm, k), lambda i, j: (i, 0)),
                  pl.BlockSpec((k, tn), lambda i, j: (0, j))],
        out_specs=pl.BlockSpec((None, tm, tn), lambda i, j: (j, i, 0)),
        out_shape=jax.ShapeDtypeStruct((n // tn, m, tn), F32),
        compiler_params=_params("parallel", "parallel"),
        name="matmul_slabs",
    )(x, w)


def _qkv_heads_kernel(x_ref, w_ref, o_ref, kh_ref, vh_ref):
    j = pl.program_id(1)
    y = jnp.dot(x_ref[...].astype(BF16), w_ref[...], preferred_element_type=F32)
    o_ref[...] = y

    def heads_out(ref):
        for h in range(ATT_HEADS):
            ref[:, h, :] = y[:, h * ATT_W:(h + 1) * ATT_W]

    pl.when(j == 1)(functools.partial(heads_out, kh_ref))
    pl.when(j == 2)(functools.partial(heads_out, vh_ref))


def qkv_project(x, w, *, tm):
    m, k = x.shape
    width = ATT_HEADS * ATT_W
    heads = jax.ShapeDtypeStruct((m, ATT_HEADS, ATT_W), F32)
    heads_spec = pl.BlockSpec((tm, ATT_HEADS, ATT_W), lambda i, j: (i, 0, 0))
    return pl.pallas_call(
        _qkv_heads_kernel,
        grid=(m // tm, 3),
        in_specs=[pl.BlockSpec((tm, k), lambda i, j: (i, 0)),
                  pl.BlockSpec((k, width), lambda i, j: (0, j))],
        out_specs=[pl.BlockSpec((None, tm, width), lambda i, j: (j, i, 0)), heads_spec, heads_spec],
        out_shape=[jax.ShapeDtypeStruct((3, m, width), F32), heads, heads],
        compiler_params=_params("parallel", "arbitrary"),
        name="qkv_project",
    )(x, w)


FLASH_ROW_GROUPS = 8


def _flash_kernel(slope_ref, wl_ref, g_ref, q_ref, k_ref, v_ref, o_ref,
                  q_s, m_s, l_s, a_s, *, tq, tk, lam_init):
    qi = pl.program_id(2)
    ki = pl.program_id(3)
    nk = pl.num_programs(3)
    q0 = qi * tq
    k0 = ki * tk

    @pl.when(ki == 0)
    def _init():
        q = q_ref[...] * (ATT_HD ** -0.5 * LOG2E)
        lane = lax.broadcasted_iota(jnp.int32, q.shape, 1)
        q_s[0:tq] = jnp.where(lane < ATT_HD, q, 0.0).astype(BF16)
        q_s[tq:2 * tq] = jnp.where(lane >= ATT_HD, q, 0.0).astype(BF16)
        m_s[...] = jnp.full(m_s.shape, NEG_INF, F32)
        l_s[...] = jnp.zeros(l_s.shape, F32)
        a_s[...] = jnp.zeros(a_s.shape, F32)

    def step(on_diagonal):
        kb = k_ref[...].astype(BF16)
        vb = jnp.concatenate([v_ref[...].astype(BF16), jnp.ones((tk, LANES), BF16)], axis=1)
        col = lax.broadcasted_iota(jnp.int32, (1, tk), 1)
        bias = (slope_ref[...] * LOG2E) * (k0 + col - q0).astype(F32)
        rg = 2 * tq // FLASH_ROW_GROUPS
        groups = [slice(g * rg, (g + 1) * rg) for g in range(FLASH_ROW_GROUPS)]
        scores = [_nt_dot(q_s[rows, :], kb) for rows in groups]
        for g, rows in enumerate(groups):
            s = scores[g] + bias
            if on_diagonal:
                r = lax.broadcasted_iota(jnp.int32, (rg, tk), 0) + (g * rg) % tq
                c = lax.broadcasted_iota(jnp.int32, (rg, tk), 1)
                s = jnp.where(c > r, NEG_INF, s)
            m_old = m_s[rows, :]
            m_new = jnp.maximum(m_old, jnp.max(s, axis=1, keepdims=True))
            alpha = jnp.exp2(m_old - m_new)
            p = jnp.exp2(s - jnp.tile(m_new, (1, tk // LANES)))
            pv = jnp.dot(p.astype(BF16), vb, preferred_element_type=F32)
            l_s[rows, :] = alpha * l_s[rows, :] + pv[:, ATT_W:]
            a_s[rows, :] = alpha * a_s[rows, :] + pv[:, :ATT_W]
            m_s[rows, :] = m_new

    pl.when(ki < qi)(functools.partial(step, False))
    pl.when(ki == qi)(functools.partial(step, True))

    @pl.when(ki == nk - 1)
    def _finish():
        lam = _diff_lambda(wl_ref[...], lam_init)
        w = a_s[...] / l_s[...]
        o = w[0:tq] - lam * w[tq:2 * tq]
        ms = jnp.mean(o * o, axis=1, keepdims=True)
        o_ref[...] = o * lax.rsqrt(ms + SUBLN_EPS) * g_ref[...] * (1.0 - lam_init)


def flash_prompt(qkv, slopes, w_lam, subln_g, *, batch, seq, tq, tk, lam_init):
    assert tq == tk
    nq = seq // tq
    nkb = seq // tk

    def kv_row(b, qi, ki):
        return b * nkb + jnp.minimum(ki, qi)

    kern = functools.partial(_flash_kernel, tq=tq, tk=tk, lam_init=lam_init)
    return pl.pallas_call(
        kern,
        grid=(batch, ATT_HEADS, nq, nkb),
        in_specs=[
            pl.BlockSpec((None, 1, tk), lambda b, h, qi, ki: (h, 0, 0)),
            pl.BlockSpec((4, ATT_HD), lambda b, h, qi, ki: (0, 0)),
            pl.BlockSpec((1, ATT_W), lambda b, h, qi, ki: (0, 0)),
            pl.BlockSpec((None, tq, ATT_W), lambda b, h, qi, ki: (0, b * nq + qi, h)),
            pl.BlockSpec((None, tk, ATT_W), lambda b, h, qi, ki: (1, kv_row(b, qi, ki), h)),
            pl.BlockSpec((None, tk, ATT_W), lambda b, h, qi, ki: (2, kv_row(b, qi, ki), h)),
        ],
        out_specs=pl.BlockSpec((tq, ATT_W), lambda b, h, qi, ki: (b * nq + qi, h)),
        out_shape=jax.ShapeDtypeStruct((batch * seq, ATT_HEADS * ATT_W), F32),
        scratch_shapes=[
            pltpu.VMEM((2 * tq, ATT_W), BF16), pltpu.VMEM((2 * tq, LANES), F32),
            pltpu.VMEM((2 * tq, LANES), F32), pltpu.VMEM((2 * tq, ATT_W), F32),
        ],
        compiler_params=_params("parallel", "parallel", "parallel", "arbitrary"),
        name="flash_prompt",
    )(slopes, w_lam, subln_g, qkv, qkv, qkv)


DECODE_PAGES_PER_STEP = 4


def _decode_kernel(pt_ref, q_ref, kn_ref, vn_ref, *rest, n_grp, page, past, lam_init):
    del pt_ref
    k_refs = rest[:n_grp]
    v_refs = rest[n_grp:2 * n_grp]
    bsel_ref, slope_ref, alibi_ref, g_ref, wl_ref, o_ref, m_s, l_s, aa_s, ab_s = rest[2 * n_grp:]
    pg = pl.program_id(1)
    n_steps = pl.num_programs(1)
    nh = ATT_HEADS
    q8 = q_ref[...] * (ATT_HD ** -0.5 * LOG2E)

    def half_sums(prod):
        return jnp.dot(prod.astype(BF16), bsel_ref[...], preferred_element_type=F32)

    def swap_halves(x):
        return pltpu.roll(x, ATT_HD, x.ndim - 1)

    @pl.when(pg == 0)
    def _init():
        m_s[...] = jnp.full(m_s.shape, NEG_INF, F32)
        for ref in (l_s, aa_s, ab_s):
            ref[...] = jnp.zeros(ref.shape, F32)

    slope = slope_ref[...]
    logits, shifts = [], []
    m_new = m_s[...]
    for g in range(n_grp):
        prod = (k_refs[g][...] * q8[None]).reshape(page * nh, ATT_W)
        s3 = half_sums(prod).reshape(page, nh, ATT_W) + alibi_ref[...]
        shift = slope * (past - (pg * n_grp + g) * page).astype(F32)
        m_new = jnp.maximum(m_new, jnp.max(s3, axis=0) - shift)
        logits.append(s3)
        shifts.append(shift)
    alpha = jnp.exp2(m_s[...] - m_new)
    l = alpha * l_s[...]
    acc_a = alpha * aa_s[...]
    acc_b = swap_halves(alpha) * ab_s[...]
    for g in range(n_grp):
        pe = jnp.exp2(logits[g] - (m_new + shifts[g])[None])
        v3 = v_refs[g][...]
        l = l + jnp.sum(pe, axis=0)
        acc_a = acc_a + jnp.sum(pe * v3, axis=0)
        acc_b = acc_b + jnp.sum(swap_halves(pe) * v3, axis=0)
    m_s[...] = m_new
    l_s[...] = l
    aa_s[...] = acc_a
    ab_s[...] = acc_b

    @pl.when(pg == n_steps - 1)
    def _finish():
        s_self = half_sums(q8 * kn_ref[...])
        m_n = jnp.maximum(m_new, s_self)
        al = jnp.exp2(m_new - m_n)
        p_self = jnp.exp2(s_self - m_n)
        lf = al * l + p_self
        vn = vn_ref[...]
        fa = al * acc_a + p_self * vn
        fb = swap_halves(al) * acc_b + swap_halves(p_self) * vn
        first = lax.broadcasted_iota(jnp.int32, fa.shape, 1) < ATT_HD
        lf_sw = swap_halves(lf)
        o1 = jnp.where(first, fa, fb) / jnp.where(first, lf, lf_sw)
        o2 = jnp.where(first, fb, fa) / jnp.where(first, lf_sw, lf)
        lam = _diff_lambda(wl_ref[...], lam_init)
        d = o1 - lam * o2
        ms = jnp.mean(d * d, axis=1, keepdims=True)
        o_ref[...] = d * lax.rsqrt(ms + SUBLN_EPS) * g_ref[...] * (1.0 - lam_init)


def decode_sample(page_table, q, k_new, v_new, cache_k, cache_v, w_lam, subln_g, *, layer, lam_init):
    bs, n_pages = page_table.shape
    page = cache_k.shape[2]
    nh = ATT_HEADS
    past = n_pages * page
    n_grp = math.gcd(n_pages, DECODE_PAGES_PER_STEP)
    half = jnp.arange(ATT_W)[:, None] // ATT_HD == jnp.arange(ATT_W)[None, :] // ATT_HD
    bsel = half.astype(BF16)
    slopes = jnp.exp2(-8.0 * jnp.arange(1, nh + 1, dtype=F32) / nh) * LOG2E
    slope = jnp.broadcast_to(slopes[:, None], (nh, ATT_W))
    alibi = jnp.arange(page, dtype=F32)[:, None, None] * slope[None]

    row_spec = pl.BlockSpec((None, nh, ATT_W), lambda b, p, pt: (b, 0, 0))
    const = lambda shape: pl.BlockSpec(shape, lambda b, p, pt: (0,) * len(shape))

    def page_spec(g):
        return pl.BlockSpec((None, None, page, nh, ATT_W),
                            lambda b, p, pt: (layer, pt[b, p * n_grp + g], 0, 0, 0))

    pages = [page_spec(g) for g in range(n_grp)]
    kern = functools.partial(_decode_kernel, n_grp=n_grp, page=page, past=past, lam_init=lam_init)
    return pl.pallas_call(
        kern,
        grid_spec=pltpu.PrefetchScalarGridSpec(
            num_scalar_prefetch=1,
            grid=(bs, n_pages // n_grp),
            in_specs=[row_spec, row_spec, row_spec] + pages + pages + [
                const((ATT_W, ATT_W)), const((nh, ATT_W)), const((page, nh, ATT_W)),
                const((1, ATT_W)), const((4, ATT_HD))],
            out_specs=pl.BlockSpec((None, nh, ATT_W), lambda b, p, pt: (b, 0, 0)),
            scratch_shapes=[pltpu.VMEM((nh, ATT_W), F32)] * 4,
        ),
        out_shape=jax.ShapeDtypeStruct((bs, nh, ATT_W), F32),
        compiler_params=_params("parallel", "arbitrary"),
        name="decode_sample",
    )(page_table, q, k_new, v_new, *([cache_k] * n_grp), *([cache_v] * n_grp),
      bsel, slope, alibi, subln_g[None, :], w_lam)


def _mm_res_ln_kernel(xp_ref, xt_ref, w_ref, rp_ref, rt_ref, g_ref, b_ref, o_ref, *, alpha, n_prompt_tiles):
    tail = pl.program_id(0) >= n_prompt_tiles
    x = jnp.where(tail, xt_ref[...], xp_ref[...])
    res = jnp.where(tail, rt_ref[...], rp_ref[...])
    y = jnp.dot(x.astype(BF16), w_ref[...], preferred_element_type=F32)
    o_ref[...] = _layer_norm(alpha * res + y, g_ref[...], b_ref[...])


def mm_res_ln(x_prompt, x_tail, w, res_prompt, res_tail, g, b, *, alpha, tm):
    tp, k = x_prompt.shape
    n = w.shape[1]
    n_p = tp // tm
    prompt = lambda width: pl.BlockSpec((tm, width), lambda i: (jnp.minimum(i, n_p - 1), 0))
    full = lambda shape: pl.BlockSpec(shape, lambda i: (0, 0))
    return pl.pallas_call(
        functools.partial(_mm_res_ln_kernel, alpha=alpha, n_prompt_tiles=n_p),
        grid=(n_p + 1,),
        in_specs=[prompt(k), full((tm, k)), full((k, n)), prompt(n), full((tm, n)),
                  full((1, n)), full((1, n))],
        out_specs=pl.BlockSpec((tm, n), lambda i: (i, 0)),
        out_shape=jax.ShapeDtypeStruct((tp + tm, n), F32),
        compiler_params=_params("parallel"),
        name="mm_res_ln",
    )(x_prompt, x_tail, w, res_prompt, res_tail, g[None, :], b[None, :])


CARRY_ROWS = 8


def _conv_kernel(*refs, alpha, tm, chained):
    if chained:
        x_ref, win_ref, cw_ref, wout_ref, g_ref, b_ref, o_ref, u_ref, carry_s = refs
    else:
        x_ref, l0_ref, l1_ref, win_ref, cw_ref, wout_ref, g_ref, b_ref, joint_ref, o_ref, u_ref = refs
        del joint_ref
    d = x_ref.shape[1]
    x = x_ref[...]
    bch = jnp.dot(x.astype(BF16), win_ref[...], preferred_element_type=F32)
    b_g = bch[:, 0:d]
    u = bch[:, d:2 * d] * bch[:, 2 * d:3 * d]
    cw = cw_ref[...]
    if chained:
        i = pl.program_id(1)

        @pl.when(i == 0)
        def _zero_left():
            carry_s[...] = jnp.zeros(carry_s.shape, F32)

        prev = carry_s[...]
        row = lax.broadcasted_iota(jnp.int32, u.shape, 0)
        last = prev[CARRY_ROWS - 1:CARRY_ROWS]
        u1 = jnp.where(row == 0, last, pltpu.roll(u, 1, 0))
        u2 = jnp.where(row == 0, prev[CARRY_ROWS - 2:CARRY_ROWS - 1],
                       jnp.where(row == 1, last, pltpu.roll(u, 2, 0)))
        tail = u[tm - CARRY_ROWS:tm]
        carry_s[...] = tail
        u_ref[...] = tail
    else:
        u2 = l0_ref[...]
        u1 = l1_ref[...]
        u_ref[...] = u
    z = cw[0:1] * u2 + cw[1:2] * u1 + cw[2:3] * u
    y = jnp.dot((b_g * z).astype(BF16), wout_ref[...], preferred_element_type=F32)
    o_ref[...] = _layer_norm(alpha * x + y, g_ref[...], b_ref[...])


def conv_prompt(x, w_in, conv_w, w_out, g, b, *, batch, seq, alpha, tm):
    t_rows, d = x.shape
    nt = seq // tm
    rows = pl.BlockSpec((tm, d), lambda bi, i: (bi * nt + i, 0))
    full = lambda shape: pl.BlockSpec(shape, lambda bi, i: (0, 0))
    return pl.pallas_call(
        functools.partial(_conv_kernel, alpha=alpha, tm=tm, chained=True),
        grid=(batch, nt),
        in_specs=[rows, full((d, 3 * d)), full((CONV_W, d)), full((d, d)), full((1, d)), full((1, d))],
        out_specs=[rows, pl.BlockSpec((None, CARRY_ROWS, d), lambda bi, i: (bi, 0, 0))],
        out_shape=[jax.ShapeDtypeStruct((t_rows, d), F32),
                   jax.ShapeDtypeStruct((batch, CARRY_ROWS, d), F32)],
        scratch_shapes=[pltpu.VMEM((CARRY_ROWS, d), F32)],
        compiler_params=_params("parallel", "arbitrary"),
        name="conv_prompt",
    )(x, w_in, conv_w, w_out, g[None, :], b[None, :])


def conv_sample(x, joint_out, left0, left1, w_in, conv_w, w_out, g, b, *, alpha, tm):
    t_rows, d = x.shape
    last = t_rows // tm - 1
    tile = pl.BlockSpec((tm, d), lambda i: (last, 0))
    full = lambda shape: pl.BlockSpec(shape, lambda i: (0, 0))
    return pl.pallas_call(
        functools.partial(_conv_kernel, alpha=alpha, tm=tm, chained=False),
        grid=(1,),
        in_specs=[tile, full((tm, d)), full((tm, d)), full((d, 3 * d)), full((CONV_W, d)),
                  full((d, d)), full((1, d)), full((1, d)), pl.BlockSpec(memory_space=pl.ANY)],
        out_specs=[tile, full((tm, d))],
        out_shape=[jax.ShapeDtypeStruct((t_rows, d), F32), jax.ShapeDtypeStruct((tm, d), F32)],
        input_output_aliases={8: 0},
        compiler_params=_params("arbitrary"),
        name="conv_sample",
    )(x, left0, left1, w_in, conv_w, w_out, g[None, :], b[None, :], joint_out)


PACK = 4 // jnp.dtype(BF16).itemsize


def _pack_rows(x):
    return pltpu.bitcast(x.astype(BF16), jnp.uint32)


def _unpack_rows(x):
    return pltpu.bitcast(x, BF16)


def _replicate_word(x):
    if PACK == 1:
        return pltpu.bitcast(x, jnp.uint32)
    hi = pltpu.bitcast(x.astype(BF16).astype(F32), jnp.uint32)
    return hi | (hi >> 16)


def _top_values(x, count, store, want_rank=False):
    rank = jnp.full(x.shape, float(count), F32) if want_rank else None
    for r in range(count):
        mx = jnp.max(x, axis=0, keepdims=True)
        store(r, mx)
        hit = x == mx
        if want_rank:
            rank = jnp.where(hit, float(r), rank)
        if r + 1 < count:
            x = jnp.where(hit, -jnp.inf, x)
    return rank


def _select_kernel(q_ref, keys_ref, cnt_ref, e1_ref, r2_ref, e2_ref, s1_s, sv_s):
    k = PEER_TOPK
    dh = keys_ref.shape[2]
    per_slab = q_ref.shape[2] // dh
    for hc in range(2 * PEER_HEADS):
        h, second = divmod(hc, 2)
        qb = q_ref[hc // per_slab, :, (hc % per_slab) * dh:(hc % per_slab + 1) * dh].astype(BF16)
        s = _nt_dot(keys_ref[hc], qb)

        def store(r, mx, hc=hc):
            sv_s[hc, r:r + 1, :] = mx

        rank = _top_values(s, k, store, want_rank=bool(second))
        if second:
            r2_ref[0, h] = _pack_rows(rank)
            e2_ref[0, h] = _pack_rows(jnp.exp(s - sv_s[hc, 0:1, :]))
        else:
            s1_s[h] = s

    for h in range(PEER_HEADS):
        sv1 = sv_s[2 * h]
        sv2 = sv_s[2 * h + 1]
        sub = lax.broadcasted_iota(jnp.int32, (SUBLANES, LANES), 0)
        pieces = [sv1[0:1] + sv2]
        for a in range(2, SUBLANES + 1):
            sums = sv1[a - 1:a] + sv2[0:SUBLANES]
            pieces.append(sums if k // a >= SUBLANES else jnp.where(sub < k // a, sums, -jnp.inf))
        pieces.append(sv1[SUBLANES:k] + sv2[0:1])
        cand = jnp.concatenate(pieces, axis=0)
        tau_box = []
        _top_values(cand, k, lambda r, mx: tau_box.append(mx))
        tau = tau_box[-1]
        top = sv1[0:1] + sv2[0:1]
        z = jnp.sum(jnp.where(cand >= tau, jnp.exp(cand - top), 0.0), axis=0, keepdims=True)
        s1 = s1_s[h]
        cnt = jnp.zeros(s1.shape, F32)
        for b in range(k // 2):
            cnt = cnt + jnp.where(s1 + sv2[b:b + 1] >= tau, 1.0, 0.0)
        cnt_best = jnp.zeros((1, LANES), F32)
        for b in range(k // 2, k):
            cnt_best = cnt_best + jnp.where(sv1[0:1] + sv2[b:b + 1] >= tau, 1.0, 0.0)
        cnt_ref[0, h] = _replicate_word(jnp.where(s1 == sv1[0:1], cnt + cnt_best, cnt))
        e1_ref[0, h] = _replicate_word(jnp.exp(s1 - sv1[0:1]) * (0.5 / z))


def peer_select(q_slabs, keys):
    n_slab, t, slab_w = q_slabs.shape
    nhc, _, dh = keys.shape
    nchunk = t // LANES
    def out(rows, dtype):
        spec = pl.BlockSpec((1, PEER_HEADS, rows, LANES), lambda i: (i, 0, 0, 0))
        return spec, jax.ShapeDtypeStruct((nchunk, PEER_HEADS, rows, LANES), dtype)

    outs = [out(N_KEYS, jnp.uint32), out(N_KEYS, jnp.uint32),
            out(N_KEYS // PACK, jnp.uint32), out(N_KEYS // PACK, jnp.uint32)]
    return pl.pallas_call(
        _select_kernel,
        grid=(nchunk,),
        in_specs=[pl.BlockSpec((n_slab, LANES, slab_w), lambda i: (0, i, 0)),
                  pl.BlockSpec((nhc, N_KEYS, dh), lambda i: (0, 0, 0))],
        out_specs=[spec for spec, _ in outs],
        out_shape=[sds for _, sds in outs],
        scratch_shapes=[pltpu.VMEM((PEER_HEADS, N_KEYS, LANES), F32),
                        pltpu.VMEM((nhc, PEER_TOPK, LANES), F32)],
        compiler_params=_params("parallel"),
        name="peer_select",
    )(q_slabs, keys)


GATE_ROWS = 16
MXU_PIECES = 2
REGIONS_PER_HALF = 1


def _gelu_x2(a):
    return a * (1.0 + lax.erf(a * (2.0 ** -0.5)))


def _peer_dense_kernel(xt_ref, u_ref, vt_ref, cnt_ref, e1_ref, r2_ref, e2_ref, res_ref,
                       g_ref, b_ref, o_ref, a0_s, a1_s, w0_s, w1_s, acc_s, *, alpha, te, tt, n_tiles):
    gstep = pl.program_id(1)
    n_steps = pl.num_programs(1)
    n_i = te // N_KEYS

    @pl.when(gstep == 0)
    def _init():
        for ref in (a0_s, a1_s, w0_s, w1_s, acc_s):
            ref[...] = jnp.zeros(ref.shape, ref.dtype)

    assert n_i == 4 and tt % (2 * LANES) == 0
    th = tt // 2

    def gate_block(iis, tc, rbs, i0, a_ref, w_ref):
        lanes = slice(tc * LANES, (tc + 1) * LANES)

        def row(ref, ii, h):
            r = ref[tc, h, pl.ds(i0 + ii, 1), :]
            return _unpack_rows(jnp.broadcast_to(r, (GATE_ROWS // PACK, LANES)))

        cnt = {(ii, h): row(cnt_ref, ii, h) for ii in iis for h in range(PEER_HEADS)}
        e1 = {(ii, h): row(e1_ref, ii, h) for ii in iis for h in range(PEER_HEADS)}
        zero = jnp.zeros((GATE_ROWS, LANES), BF16)
        for rb in rbs:
            keys = slice(rb * GATE_ROWS // PACK, (rb + 1) * GATE_ROWS // PACK)
            gate = {ii: zero for ii in iis}
            for h in range(PEER_HEADS):
                r2 = _unpack_rows(r2_ref[tc, h, keys, :])
                e2 = _unpack_rows(e2_ref[tc, h, keys, :])
                for ii in iis:
                    gate[ii] = gate[ii] + jnp.where(r2 < cnt[ii, h], e2, zero) * e1[ii, h]
            for ii in iis:
                out_rows = slice(ii * N_KEYS + rb * GATE_ROWS, ii * N_KEYS + (rb + 1) * GATE_ROWS)
                w_ref[out_rows, lanes] = gate[ii] * _gelu_x2(a_ref[out_rows, lanes].astype(BF16))

    def half_step(tile, a_src, w_dst, w_src, vt0, a_dst, u0, region_base):
        i0 = jnp.clip(tile, 0, n_tiles - 1) * n_i
        n_tc = tt // LANES
        def quarter(r):
            iis = (2 * (r // 2), 2 * (r // 2) + 1)
            n_rb = N_KEYS // GATE_ROWS
            units = [(tc, range(part * n_rb // 2, (part + 1) * n_rb // 2))
                     for tc in range((r % 2) * n_tc // 2, (r % 2 + 1) * n_tc // 2) for part in range(2)]
            for piece in range(MXU_PIECES):
                if r < 2:
                    tok = slice(r * th, (r + 1) * th)
                    rows = slice(piece * (d // MXU_PIECES), (piece + 1) * (d // MXU_PIECES))
                    vt = _unpack_rows(vt_ref[rows.start // PACK:rows.stop // PACK, vt0:vt0 + te])
                    acc_s[rows, tok] += jnp.dot(vt, w_src[:, tok], preferred_element_type=F32)
                else:
                    tok = slice((r - 2) * th, (r - 1) * th)
                    rows = slice(piece * (te // MXU_PIECES), (piece + 1) * (te // MXU_PIECES))
                    u = _unpack_rows(u_ref[(u0 + rows.start) // PACK:(u0 + rows.stop) // PACK, :])
                    a_dst[rows, tok] = jnp.dot(u, xt_ref[:, tok], preferred_element_type=F32)
                for tc, rbs in units[piece * len(units) // MXU_PIECES:(piece + 1) * len(units) // MXU_PIECES]:
                    gate_block(iis, tc, rbs, i0, a_src, w_dst)

        per_region = n_i // REGIONS_PER_HALF
        for region in range(REGIONS_PER_HALF):
            @pl.when(gstep < n_steps + region_base + region)
            def _region(region=region):
                for r in range(region * per_region, (region + 1) * per_region):
                    quarter(r)

    d = acc_s.shape[0]
    half_step(2 * gstep - 1, a1_s, w1_s, w0_s, 0, a0_s, 0, 0)
    half_step(2 * gstep, a0_s, w0_s, w1_s, te, a1_s, te, REGIONS_PER_HALF)

    @pl.when(gstep == n_steps - 1)
    def _finish():
        y = acc_s[...].T
        o_ref[...] = _layer_norm(alpha * res_ref[...] + y, g_ref[...], b_ref[...])


def _pack_table_kernel(x_ref, o_ref, *, transpose):
    x = x_ref[...]
    o_ref[...] = _pack_rows(x.T if transpose else x)


def pack_table(tables, layer, *, transpose, tile=1024):
    _, rows, cols = tables.shape
    if transpose:
        out_shape = (cols // PACK, rows)
        out_spec = pl.BlockSpec((cols // PACK, tile), lambda i: (0, i))
    else:
        out_shape = (rows // PACK, cols)
        out_spec = pl.BlockSpec((tile // PACK, cols), lambda i: (i, 0))
    return pl.pallas_call(
        functools.partial(_pack_table_kernel, transpose=transpose),
        grid=(rows // tile,),
        in_specs=[pl.BlockSpec((None, tile, cols), lambda i: (layer, i, 0))],
        out_specs=out_spec,
        out_shape=jax.ShapeDtypeStruct(out_shape, jnp.uint32),
        compiler_params=_params("parallel"),
        name="pack_table",
    )(tables)


def peer_dense(xt, u, vt, sel, res, g, b, *, alpha, tt, te):
    d, t = xt.shape
    n_exp = u.shape[0] * PACK
    n_tiles = n_exp // te
    assert n_tiles % 2 == 0
    n_pairs = n_tiles // 2
    nchunk = tt // LANES
    sel_specs = [pl.BlockSpec((nchunk,) + a.shape[1:], lambda ti, s: (ti, 0, 0, 0)) for a in sel]
    full = lambda shape: pl.BlockSpec(shape, lambda ti, s: (0, 0))
    u_spec = pl.BlockSpec((2 * te // PACK, d), lambda ti, s: (jnp.minimum(s, n_pairs - 1), 0))
    vt_spec = pl.BlockSpec((d // PACK, 2 * te), lambda ti, s: (0, jnp.maximum(s - 1, 0)))
    return pl.pallas_call(
        functools.partial(_peer_dense_kernel, alpha=alpha, te=te, tt=tt, n_tiles=n_tiles),
        grid=(t // tt, n_pairs + 1),
        in_specs=[pl.BlockSpec((d, tt), lambda ti, s: (0, ti)), u_spec, vt_spec,
                  *sel_specs,
                  pl.BlockSpec((tt, d), lambda ti, s: (ti, 0)),
                  full((1, d)), full((1, d))],
        out_specs=pl.BlockSpec((tt, d), lambda ti, s: (ti, 0)),
        out_shape=jax.ShapeDtypeStruct((t, d), F32),
        scratch_shapes=[pltpu.VMEM((te, tt), F32), pltpu.VMEM((te, tt), F32),
                        pltpu.VMEM((te, tt), BF16), pltpu.VMEM((te, tt), BF16),
                        pltpu.VMEM((d, tt), F32)],
        compiler_params=_params("parallel", "arbitrary"),
        name="peer_dense",
    )(xt, u, vt, *sel, res, g[None, :], b[None, :])


def _ple_kernel(h_ref, pp_ref, pt_ref, wg_ref, wp_ref, *o_refs, n_prompt_tiles):
    i = pl.program_id(0)
    tail = i >= n_prompt_tiles
    h = h_ref[...]
    p = jnp.where(tail, pt_ref[...], pp_ref[...])
    gate = jax.nn.sigmoid(jnp.dot(h.astype(BF16), wg_ref[...], preferred_element_type=F32))
    proj = jnp.dot(p.astype(BF16), wp_ref[...], preferred_element_type=F32)
    out = h + gate * proj
    if len(o_refs) == 1:
        o_refs[0][...] = out
    else:
        op_ref, ot_ref = o_refs

        @pl.when(jnp.logical_not(tail))
        def _():
            op_ref[...] = out

        @pl.when(tail)
        def _():
            ot_ref[...] = out


def ple_add(h, p_prompt, layer, p_tail, wg, wp, *, tm, split):
    m, d = h.shape
    pd = p_tail.shape[1]
    n_p = m // tm - 1
    prompt_tile = lambda i: jnp.minimum(i, n_p - 1)
    full = lambda shape: pl.BlockSpec(shape, lambda i: (0, 0))
    if split:
        out_specs = [pl.BlockSpec((tm, d), lambda i: (prompt_tile(i), 0)), full((tm, d))]
        out_shape = [jax.ShapeDtypeStruct((n_p * tm, d), F32), jax.ShapeDtypeStruct((tm, d), F32)]
    else:
        out_specs = pl.BlockSpec((tm, d), lambda i: (i, 0))
        out_shape = jax.ShapeDtypeStruct((m, d), F32)
    return pl.pallas_call(
        functools.partial(_ple_kernel, n_prompt_tiles=n_p),
        grid=(m // tm,),
        in_specs=[pl.BlockSpec((tm, d), lambda i: (i, 0)),
                  pl.BlockSpec((None, tm, pd), lambda i: (layer, prompt_tile(i), 0)),
                  full((tm, pd)), full((d, d)), full((pd, d))],
        out_specs=out_specs,
        out_shape=out_shape,
        compiler_params=_params("arbitrary"),
        name="ple_add",
    )(h, p_prompt, p_tail, wg, wp)


TOKEN_TILE = 512
PEER_EXPERT_TILE = 512


def _largest_tile(n, candidates):
    for c in candidates:
        if n % c == 0:
            return c
    raise ValueError(f"no tile in {candidates} divides {n}")


def kernel(x_prompt, x_sample, cache_k, cache_v, state_conv, page_table, p_prompt, p_sample,
           ln_g, ln_b, w_attn_qkv, w_attn_lambda, attn_subln_g, w_attn_o,
           w_conv_in, conv_w, w_conv_out, w_peer_q, peer_keys, peer_u, peer_v,
           w_ple_gate, w_ple_proj):
    batch, seq, d = x_prompt.shape
    bs = x_sample.shape[0]
    assert x_sample.shape[1] == 1
    depth = ln_g.shape[0]
    tp = batch * seq
    t = tp + bs
    tm = TOKEN_TILE
    t_pad = -(-t // tm) * tm
    alpha = (2 * depth) ** 0.25
    width = ATT_HEADS * ATT_W
    tseq = _largest_tile(seq, (512, 256, 128))
    tflash = _largest_tile(seq, (1024, 512, 256, 128))

    assert tp % tm == 0 and t_pad == tp + tm

    def tail_tile(sample_rows):
        pad = jnp.zeros((tm - bs, sample_rows.shape[1]), F32)
        return jnp.concatenate([sample_rows, pad], axis=0)

    h = None
    slopes = jnp.exp2(-8.0 * jnp.arange(1, ATT_HEADS + 1, dtype=F32) / ATT_HEADS)
    slopes_b = jnp.broadcast_to(slopes[:, None, None], (ATT_HEADS, 1, tflash))

    kp_l, vp_l, ks_l, vs_l, cp_l, cs_l = [], [], [], [], [], []
    for i in range(depth):
        j = i // N_MIXERS
        if i % N_MIXERS == 0:
            lam_init = 0.8 - 0.6 * math.exp(-0.3 * i)
            w_qkv = w_attn_qkv[j].astype(BF16)
            rows_p = x_prompt.reshape(tp, d) if h is None else h[:tp]
            rows_s = x_sample.reshape(bs, d) if h is None else h[tp:t]
            qkv, k_heads, v_heads = qkv_project(rows_p, w_qkv, tm=tseq)
            qkv_s = matmul_slabs(rows_s, w_qkv, tm=bs, tn=width).reshape(3, bs, ATT_HEADS, ATT_W)
            g_sub = attn_subln_g[j]
            o_p = flash_prompt(qkv, slopes_b, w_attn_lambda[j], g_sub[None, :], batch=batch, seq=seq,
                               tq=tflash, tk=tflash, lam_init=lam_init)
            o_s = decode_sample(page_table, qkv_s[0], qkv_s[1], qkv_s[2], cache_k, cache_v,
                                w_attn_lambda[j], g_sub, layer=j, lam_init=lam_init)
            h = mm_res_ln(o_p, tail_tile(o_s.reshape(bs, width)), w_attn_o[j].astype(BF16),
                          rows_p, tail_tile(rows_s), ln_g[i, 0], ln_b[i, 0], alpha=alpha, tm=tm)
            kp_l.append(k_heads.reshape(batch, seq, ATT_HEADS, ATT_W))
            vp_l.append(v_heads.reshape(batch, seq, ATT_HEADS, ATT_W))
            ks_l.append(qkv_s[1].reshape(bs, 1, ATT_HEADS, ATT_W))
            vs_l.append(qkv_s[2].reshape(bs, 1, ATT_HEADS, ATT_W))
        else:
            if h is None:
                h = jnp.concatenate([x_prompt.reshape(tp, d), tail_tile(x_sample.reshape(bs, d))], axis=0)
            w_in = w_conv_in[j].astype(BF16)
            w_out = w_conv_out[j].astype(BF16)
            h_new, tail = conv_prompt(h, w_in, conv_w[j], w_out, ln_g[i, 0], ln_b[i, 0],
                                      batch=batch, seq=seq, alpha=alpha, tm=tseq)
            left = state_conv[j]
            h, u_tile = conv_sample(h, h_new, tail_tile(left[:, 0]), tail_tile(left[:, 1]), w_in,
                                    conv_w[j], w_out, ln_g[i, 0], ln_b[i, 0], alpha=alpha, tm=tm)
            cp_l.append(tail[:, CARRY_ROWS - (CONV_W - 1):])
            cs_l.append(jnp.stack([left[:, 1], u_tile[:bs]], axis=1))

        n_hc = 2 * PEER_HEADS
        dh = peer_keys.shape[-1]
        q_slabs = matmul_slabs(h, w_peer_q[i].astype(BF16), tm=tm, tn=d)
        keys = peer_keys[i].reshape(n_hc, N_KEYS, dh).astype(BF16)
        sel = peer_select(q_slabs, keys)
        u_packed = pack_table(peer_u, i, transpose=False)
        vt_packed = pack_table(peer_v, i, transpose=True)
        h = peer_dense(h.T.astype(BF16), u_packed, vt_packed, sel, h,
                       ln_g[i, 1], ln_b[i, 1], alpha=alpha, tt=tm, te=PEER_EXPERT_TILE)

        last = i == depth - 1
        out = ple_add(h, p_prompt.reshape(depth, tp, -1), i, tail_tile(p_sample[i].reshape(bs, -1)),
                      w_ple_gate[i].astype(BF16), w_ple_proj[i].astype(BF16), tm=tm, split=last)
        if not last:
            h = out
    h_prompt, h_tail = out

    return (h_prompt.reshape(batch, seq, d), h_tail[:bs].reshape(bs, 1, d),
            jnp.stack(kp_l), jnp.stack(vp_l), jnp.stack(ks_l), jnp.stack(vs_l),
            jnp.stack(cp_l), jnp.stack(cs_l))
```

```python
import functools
import math

import jax
import jax.numpy as jnp
from jax import lax
from jax.experimental import pallas as pl
from jax.experimental.pallas import tpu as pltpu

BF16 = jnp.bfloat16
F32 = jnp.float32

ATT_HEADS = 8
ATT_HD = 64
ATT_W = 2 * ATT_HD
N_MIXERS = 2
CONV_W = 3
PEER_HEADS = 8
N_KEYS = 128
PEER_TOPK = 16
LN_EPS = 1e-5
SUBLN_EPS = 1e-5
NEG_INF = -1e30
LOG2E = math.log2(math.e)
LANES = 128
SUBLANES = 8
VMEM_LIMIT = 56 * 1024 * 1024


def _params(*sem, flags=None):
    return pltpu.CompilerParams(dimension_semantics=sem, vmem_limit_bytes=VMEM_LIMIT, flags=flags)


def _nt_dot(a, b):
    return lax.dot_general(a, b, (((1,), (1,)), ((), ())), preferred_element_type=F32)


def _layer_norm(z, g, b):
    mu = jnp.mean(z, axis=-1, keepdims=True)
    zc = z - mu
    var = jnp.mean(zc * zc, axis=-1, keepdims=True)
    return zc * lax.rsqrt(var + LN_EPS) * g + b


def _diff_lambda(wl, lam_init):
    a = jnp.sum(wl[0:1] * wl[1:2], axis=1, keepdims=True)
    b = jnp.sum(wl[2:3] * wl[3:4], axis=1, keepdims=True)
    return jnp.exp(a) - jnp.exp(b) + lam_init


def _mm_kernel(x_ref, w_ref, o_ref):
    o_ref[...] = jnp.dot(x_ref[...].astype(BF16), w_ref[...], preferred_element_type=F32)


def matmul_slabs(x, w, *, tm, tn):
    m, k = x.shape
    n = w.shape[1]
    return pl.pallas_call(
        _mm_kernel,
        grid=(m // tm, n // tn),
        in_specs=[pl.BlockSpec((tm, k), lambda i, j: (i, 0)),
                  pl.BlockSpec((k, tn), lambda i, j: (0, j))],
        out_specs=pl.BlockSpec((None, tm, tn), lambda i, j: (j, i, 0)),
        out_shape=jax.ShapeDtypeStruct((n // tn, m, tn), F32),
        compiler_params=_params("parallel", "parallel"),
        name="matmul_slabs",
    )(x, w)


def _qkv_heads_kernel(x_ref, w_ref, o_ref, kh_ref, vh_ref):
    j = pl.program_id(1)
    y = jnp.dot(x_ref[...].astype(BF16), w_ref[...], preferred_element_type=F32)
    o_ref[...] = y

    def heads_out(ref):
        for h in range(ATT_HEADS):
            ref[:, h, :] = y[:, h * ATT_W:(h + 1) * ATT_W]

    pl.when(j == 1)(functools.partial(heads_out, kh_ref))
    pl.when(j == 2)(functools.partial(heads_out, vh_ref))


def qkv_project(x, w, *, tm):
    m, k = x.shape
    width = ATT_HEADS * ATT_W
    heads = jax.ShapeDtypeStruct((m, ATT_HEADS, ATT_W), F32)
    heads_spec = pl.BlockSpec((tm, ATT_HEADS, ATT_W), lambda i, j: (i, 0, 0))
    return pl.pallas_call(
        _qkv_heads_kernel,
        grid=(m // tm, 3),
        in_specs=[pl.BlockSpec((tm, k), lambda i, j: (i, 0)),
                  pl.BlockSpec((k, width), lambda i, j: (0, j))],
        out_specs=[pl.BlockSpec((None, tm, width), lambda i, j: (j, i, 0)), heads_spec, heads_spec],
        out_shape=[jax.ShapeDtypeStruct((3, m, width), F32), heads, heads],
        compiler_params=_params("parallel", "arbitrary"),
        name="qkv_project",
    )(x, w)


FLASH_ROW_GROUPS = 8


def _flash_kernel(slope_ref, wl_ref, g_ref, q_ref, k_ref, v_ref, o_ref,
                  q_s, m_s, l_s, a_s, *, tq, tk, lam_init):
    qi = pl.program_id(2)
    ki = pl.program_id(3)
    nk = pl.num_programs(3)
    q0 = qi * tq
    k0 = ki * tk

    @pl.when(ki == 0)
    def _init():
        q = q_ref[...] * (ATT_HD ** -0.5 * LOG2E)
        lane = lax.broadcasted_iota(jnp.int32, q.shape, 1)
        q_s[0:tq] = jnp.where(lane < ATT_HD, q, 0.0).astype(BF16)
        q_s[tq:2 * tq] = jnp.where(lane >= ATT_HD, q, 0.0).astype(BF16)
        m_s[...] = jnp.full(m_s.shape, NEG_INF, F32)
        l_s[...] = jnp.zeros(l_s.shape, F32)
        a_s[...] = jnp.zeros(a_s.shape, F32)

    def step(on_diagonal):
        kb = k_ref[...].astype(BF16)
        vb = jnp.concatenate([v_ref[...].astype(BF16), jnp.ones((tk, LANES), BF16)], axis=1)
        col = lax.broadcasted_iota(jnp.int32, (1, tk), 1)
        bias = (slope_ref[...] * LOG2E) * (k0 + col - q0).astype(F32)
        rg = 2 * tq // FLASH_ROW_GROUPS
        groups = [slice(g * rg, (g + 1) * rg) for g in range(FLASH_ROW_GROUPS)]
        scores = [_nt_dot(q_s[rows, :], kb) for rows in groups]
        for g, rows in enumerate(groups):
            s = scores[g] + bias
            if on_diagonal:
                r = lax.broadcasted_iota(jnp.int32, (rg, tk), 0) + (g * rg) % tq
                c = lax.broadcasted_iota(jnp.int32, (rg, tk), 1)
                s = jnp.where(c > r, NEG_INF, s)
            m_old = m_s[rows, :]
            m_new = jnp.maximum(m_old, jnp.max(s, axis=1, keepdims=True))
            alpha = jnp.exp2(m_old - m_new)
            p = jnp.exp2(s - jnp.tile(m_new, (1, tk // LANES)))
            pv = jnp.dot(p.astype(BF16), vb, preferred_element_type=F32)
            l_s[rows, :] = alpha * l_s[rows, :] + pv[:, ATT_W:]
            a_s[rows, :] = alpha * a_s[rows, :] + pv[:, :ATT_W]
            m_s[rows, :] = m_new

    pl.when(ki < qi)(functools.partial(step, False))
    pl.when(ki == qi)(functools.partial(step, True))

    @pl.when(ki == nk - 1)
    def _finish():
        lam = _diff_lambda(wl_ref[...], lam_init)
        w = a_s[...] / l_s[...]
        o = w[0:tq] - lam * w[tq:2 * tq]
        ms = jnp.mean(o * o, axis=1, keepdims=True)
        o_ref[...] = o * lax.rsqrt(ms + SUBLN_EPS) * g_ref[...] * (1.0 - lam_init)


def flash_prompt(qkv, slopes, w_lam, subln_g, *, batch, seq, tq, tk, lam_init):
    assert tq == tk
    nq = seq // tq
    nkb = seq // tk

    def kv_row(b, qi, ki):
        return b * nkb + jnp.minimum(ki, qi)

    kern = functools.partial(_flash_kernel, tq=tq, tk=tk, lam_init=lam_init)
    return pl.pallas_call(
        kern,
        grid=(batch, ATT_HEADS, nq, nkb),
        in_specs=[
            pl.BlockSpec((None, 1, tk), lambda b, h, qi, ki: (h, 0, 0)),
            pl.BlockSpec((4, ATT_HD), lambda b, h, qi, ki: (0, 0)),
            pl.BlockSpec((1, ATT_W), lambda b, h, qi, ki: (0, 0)),
            pl.BlockSpec((None, tq, ATT_W), lambda b, h, qi, ki: (0, b * nq + qi, h)),
            pl.BlockSpec((None, tk, ATT_W), lambda b, h, qi, ki: (1, kv_row(b, qi, ki), h)),
            pl.BlockSpec((None, tk, ATT_W), lambda b, h, qi, ki: (2, kv_row(b, qi, ki), h)),
        ],
        out_specs=pl.BlockSpec((tq, ATT_W), lambda b, h, qi, ki: (b * nq + qi, h)),
        out_shape=jax.ShapeDtypeStruct((batch * seq, ATT_HEADS * ATT_W), F32),
        scratch_shapes=[
            pltpu.VMEM((2 * tq, ATT_W), BF16), pltpu.VMEM((2 * tq, LANES), F32),
            pltpu.VMEM((2 * tq, LANES), F32), pltpu.VMEM((2 * tq, ATT_W), F32),
        ],
        compiler_params=_params("parallel", "parallel", "parallel", "arbitrary"),
        name="flash_prompt",
    )(slopes, w_lam, subln_g, qkv, qkv, qkv)


DECODE_PAGES_PER_STEP = 8


def _decode_kernel(pt_ref, q_ref, kn_ref, vn_ref, *rest, n_grp, page, past, lam_init):
    del pt_ref
    k_refs = rest[:n_grp]
    v_refs = rest[n_grp:2 * n_grp]
    bsel_ref, slope_ref, alibi_ref, g_ref, wl_ref, o_ref, m_s, l_s, aa_s, ab_s = rest[2 * n_grp:]
    pg = pl.program_id(1)
    n_steps = pl.num_programs(1)
    nh = ATT_HEADS
    q8 = q_ref[...] * (ATT_HD ** -0.5 * LOG2E)

    def half_sums(prod):
        return jnp.dot(prod.astype(BF16), bsel_ref[...], preferred_element_type=F32)

    def swap_halves(x):
        return pltpu.roll(x, ATT_HD, x.ndim - 1)

    @pl.when(pg == 0)
    def _init():
        m_s[...] = jnp.full(m_s.shape, NEG_INF, F32)
        for ref in (l_s, aa_s, ab_s):
            ref[...] = jnp.zeros(ref.shape, F32)

    slope = slope_ref[...]
    logits, shifts = [], []
    m_new = m_s[...]
    for g in range(n_grp):
        prod = (k_refs[g][...] * q8[None]).reshape(page * nh, ATT_W)
        s3 = half_sums(prod).reshape(page, nh, ATT_W) + alibi_ref[...]
        shift = slope * (past - (pg * n_grp + g) * page).astype(F32)
        m_new = jnp.maximum(m_new, jnp.max(s3, axis=0) - shift)
        logits.append(s3)
        shifts.append(shift)
    alpha = jnp.exp2(m_s[...] - m_new)
    l = alpha * l_s[...]
    acc_a = alpha * aa_s[...]
    acc_b = swap_halves(alpha) * ab_s[...]
    for g in range(n_grp):
        pe = jnp.exp2(logits[g] - (m_new + shifts[g])[None])
        v3 = v_refs[g][...]
        l = l + jnp.sum(pe, axis=0)
        acc_a = acc_a + jnp.sum(pe * v3, axis=0)
        acc_b = acc_b + jnp.sum(swap_halves(pe) * v3, axis=0)
    m_s[...] = m_new
    l_s[...] = l
    aa_s[...] = acc_a
    ab_s[...] = acc_b

    @pl.when(pg == n_steps - 1)
    def _finish():
        s_self = half_sums(q8 * kn_ref[...])
        m_n = jnp.maximum(m_new, s_self)
        al = jnp.exp2(m_new - m_n)
        p_self = jnp.exp2(s_self - m_n)
        lf = al * l + p_self
        vn = vn_ref[...]
        fa = al * acc_a + p_self * vn
        fb = swap_halves(al) * acc_b + swap_halves(p_self) * vn
        first = lax.broadcasted_iota(jnp.int32, fa.shape, 1) < ATT_HD
        lf_sw = swap_halves(lf)
        o1 = jnp.where(first, fa, fb) / jnp.where(first, lf, lf_sw)
        o2 = jnp.where(first, fb, fa) / jnp.where(first, lf_sw, lf)
        lam = _diff_lambda(wl_ref[...], lam_init)
        d = o1 - lam * o2
        ms = jnp.mean(d * d, axis=1, keepdims=True)
        o_ref[...] = d * lax.rsqrt(ms + SUBLN_EPS) * g_ref[...] * (1.0 - lam_init)


def decode_sample(page_table, q, k_new, v_new, cache_k, cache_v, w_lam, subln_g, *, layer, lam_init):
    bs, n_pages = page_table.shape
    page = cache_k.shape[2]
    nh = ATT_HEADS
    past = n_pages * page
    n_grp = math.gcd(n_pages, DECODE_PAGES_PER_STEP)
    half = jnp.arange(ATT_W)[:, None] // ATT_HD == jnp.arange(ATT_W)[None, :] // ATT_HD
    bsel = half.astype(BF16)
    slopes = jnp.exp2(-8.0 * jnp.arange(1, nh + 1, dtype=F32) / nh) * LOG2E
    slope = jnp.broadcast_to(slopes[:, None], (nh, ATT_W))
    alibi = jnp.arange(page, dtype=F32)[:, None, None] * slope[None]

    row_spec = pl.BlockSpec((None, nh, ATT_W), lambda b, p, pt: (b, 0, 0))
    const = lambda shape: pl.BlockSpec(shape, lambda b, p, pt: (0,) * len(shape))

    def page_spec(g):
        return pl.BlockSpec((None, None, page, nh, ATT_W),
                            lambda b, p, pt: (layer, pt[b, p * n_grp + g], 0, 0, 0))

    pages = [page_spec(g) for g in range(n_grp)]
    kern = functools.partial(_decode_kernel, n_grp=n_grp, page=page, past=past, lam_init=lam_init)
    return pl.pallas_call(
        kern,
        grid_spec=pltpu.PrefetchScalarGridSpec(
            num_scalar_prefetch=1,
            grid=(bs, n_pages // n_grp),
            in_specs=[row_spec, row_spec, row_spec] + pages + pages + [
                const((ATT_W, ATT_W)), const((nh, ATT_W)), const((page, nh, ATT_W)),
                const((1, ATT_W)), const((4, ATT_HD))],
            out_specs=pl.BlockSpec((None, nh, ATT_W), lambda b, p, pt: (b, 0, 0)),
            scratch_shapes=[pltpu.VMEM((nh, ATT_W), F32)] * 4,
        ),
        out_shape=jax.ShapeDtypeStruct((bs, nh, ATT_W), F32),
        compiler_params=_params("parallel", "arbitrary"),
        name="decode_sample",
    )(page_table, q, k_new, v_new, *([cache_k] * n_grp), *([cache_v] * n_grp),
      bsel, slope, alibi, subln_g[None, :], w_lam)


def _mm_res_ln_kernel(xp_ref, xt_ref, w_ref, rp_ref, rt_ref, g_ref, b_ref, o_ref, *, alpha, n_prompt_tiles):
    tail = pl.program_id(0) >= n_prompt_tiles
    x = jnp.where(tail, xt_ref[...], xp_ref[...])
    res = jnp.where(tail, rt_ref[...], rp_ref[...])
    y = jnp.dot(x.astype(BF16), w_ref[...], preferred_element_type=F32)
    o_ref[...] = _layer_norm(alpha * res + y, g_ref[...], b_ref[...])


def mm_res_ln(x_prompt, x_tail, w, res_prompt, res_tail, g, b, *, alpha, tm):
    tp, k = x_prompt.shape
    n = w.shape[1]
    n_p = tp // tm
    prompt = lambda width: pl.BlockSpec((tm, width), lambda i: (jnp.minimum(i, n_p - 1), 0))
    full = lambda shape: pl.BlockSpec(shape, lambda i: (0, 0))
    return pl.pallas_call(
        functools.partial(_mm_res_ln_kernel, alpha=alpha, n_prompt_tiles=n_p),
        grid=(n_p + 1,),
        in_specs=[prompt(k), full((tm, k)), full((k, n)), prompt(n), full((tm, n)),
                  full((1, n)), full((1, n))],
        out_specs=pl.BlockSpec((tm, n), lambda i: (i, 0)),
        out_shape=jax.ShapeDtypeStruct((tp + tm, n), F32),
        compiler_params=_params("parallel"),
        name="mm_res_ln",
    )(x_prompt, x_tail, w, res_prompt, res_tail, g[None, :], b[None, :])


CARRY_ROWS = 8


def _conv_kernel(*refs, alpha, tm, chained):
    if chained:
        x_ref, win_ref, cw_ref, wout_ref, g_ref, b_ref, o_ref, u_ref, carry_s = refs
    else:
        x_ref, l0_ref, l1_ref, win_ref, cw_ref, wout_ref, g_ref, b_ref, joint_ref, o_ref, u_ref = refs
        del joint_ref
    d = x_ref.shape[1]
    x = x_ref[...]
    bch = jnp.dot(x.astype(BF16), win_ref[...], preferred_element_type=F32)
    b_g = bch[:, 0:d]
    u = bch[:, d:2 * d] * bch[:, 2 * d:3 * d]
    cw = cw_ref[...]
    if chained:
        i = pl.program_id(1)

        @pl.when(i == 0)
        def _zero_left():
            carry_s[...] = jnp.zeros(carry_s.shape, F32)

        prev = carry_s[...]
        row = lax.broadcasted_iota(jnp.int32, u.shape, 0)
        last = prev[CARRY_ROWS - 1:CARRY_ROWS]
        u1 = jnp.where(row == 0, last, pltpu.roll(u, 1, 0))
        u2 = jnp.where(row == 0, prev[CARRY_ROWS - 2:CARRY_ROWS - 1],
                       jnp.where(row == 1, last, pltpu.roll(u, 2, 0)))
        tail = u[tm - CARRY_ROWS:tm]
        carry_s[...] = tail
        u_ref[...] = tail
    else:
        u2 = l0_ref[...]
        u1 = l1_ref[...]
        u_ref[...] = u
    z = cw[0:1] * u2 + cw[1:2] * u1 + cw[2:3] * u
    y = jnp.dot((b_g * z).astype(BF16), wout_ref[...], preferred_element_type=F32)
    o_ref[...] = _layer_norm(alpha * x + y, g_ref[...], b_ref[...])


def conv_prompt(x, w_in, conv_w, w_out, g, b, *, batch, seq, alpha, tm):
    t_rows, d = x.shape
    nt = seq // tm
    rows = pl.BlockSpec((tm, d), lambda bi, i: (bi * nt + i, 0))
    full = lambda shape: pl.BlockSpec(shape, lambda bi, i: (0, 0))
    return pl.pallas_call(
        functools.partial(_conv_kernel, alpha=alpha, tm=tm, chained=True),
        grid=(batch, nt),
        in_specs=[rows, full((d, 3 * d)), full((CONV_W, d)), full((d, d)), full((1, d)), full((1, d))],
        out_specs=[rows, pl.BlockSpec((None, CARRY_ROWS, d), lambda bi, i: (bi, 0, 0))],
        out_shape=[jax.ShapeDtypeStruct((t_rows, d), F32),
                   jax.ShapeDtypeStruct((batch, CARRY_ROWS, d), F32)],
        scratch_shapes=[pltpu.VMEM((CARRY_ROWS, d), F32)],
        compiler_params=_params("parallel", "arbitrary"),
        name="conv_prompt",
    )(x, w_in, conv_w, w_out, g[None, :], b[None, :])


def conv_sample(x, joint_out, left0, left1, w_in, conv_w, w_out, g, b, *, alpha, tm):
    t_rows, d = x.shape
    last = t_rows // tm - 1
    tile = pl.BlockSpec((tm, d), lambda i: (last, 0))
    full = lambda shape: pl.BlockSpec(shape, lambda i: (0, 0))
    return pl.pallas_call(
        functools.partial(_conv_kernel, alpha=alpha, tm=tm, chained=False),
        grid=(1,),
        in_specs=[tile, full((tm, d)), full((tm, d)), full((d, 3 * d)), full((CONV_W, d)),
                  full((d, d)), full((1, d)), full((1, d)), pl.BlockSpec(memory_space=pl.ANY)],
        out_specs=[tile, full((tm, d))],
        out_shape=[jax.ShapeDtypeStruct((t_rows, d), F32), jax.ShapeDtypeStruct((tm, d), F32)],
        input_output_aliases={8: 0},
        compiler_params=_params("arbitrary"),
        name="conv_sample",
    )(x, left0, left1, w_in, conv_w, w_out, g[None, :], b[None, :], joint_out)


PACK = 4 // jnp.dtype(BF16).itemsize


def _pack_rows(x):
    return pltpu.bitcast(x.astype(BF16), jnp.uint32)


def _unpack_rows(x):
    return pltpu.bitcast(x, BF16)


def _replicate_word(x):
    if PACK == 1:
        return pltpu.bitcast(x, jnp.uint32)
    hi = pltpu.bitcast(x.astype(BF16).astype(F32), jnp.uint32)
    return hi | (hi >> 16)


def _top_values(x, count, store, want_rank=False):
    rank = jnp.full(x.shape, float(count), F32) if want_rank else None
    for r in range(count):
        mx = jnp.max(x, axis=0, keepdims=True)
        store(r, mx)
        hit = x == mx
        if want_rank:
            rank = jnp.where(hit, float(r), rank)
        if r + 1 < count:
            x = jnp.where(hit, -jnp.inf, x)
    return rank


def _select_kernel(q_ref, keys_ref, cnt_ref, e1_ref, r2_ref, e2_ref, s1_s, sv_s):
    k = PEER_TOPK
    dh = keys_ref.shape[2]
    per_slab = q_ref.shape[2] // dh
    for hc in range(2 * PEER_HEADS):
        h, second = divmod(hc, 2)
        qb = q_ref[hc // per_slab, :, (hc % per_slab) * dh:(hc % per_slab + 1) * dh].astype(BF16)
        s = _nt_dot(keys_ref[hc], qb)

        def store(r, mx, hc=hc):
            sv_s[hc, r:r + 1, :] = mx

        rank = _top_values(s, k, store, want_rank=bool(second))
        if second:
            r2_ref[0, h] = _pack_rows(rank)
            e2_ref[0, h] = _pack_rows(jnp.exp(s - sv_s[hc, 0:1, :]))
        else:
            s1_s[h] = s

    for h in range(PEER_HEADS):
        sv1 = sv_s[2 * h]
        sv2 = sv_s[2 * h + 1]
        sub = lax.broadcasted_iota(jnp.int32, (SUBLANES, LANES), 0)
        pieces = [sv1[0:1] + sv2]
        for a in range(2, SUBLANES + 1):
            sums = sv1[a - 1:a] + sv2[0:SUBLANES]
            pieces.append(sums if k // a >= SUBLANES else jnp.where(sub < k // a, sums, -jnp.inf))
        pieces.append(sv1[SUBLANES:k] + sv2[0:1])
        cand = jnp.concatenate(pieces, axis=0)
        tau_box = []
        _top_values(cand, k, lambda r, mx: tau_box.append(mx))
        tau = tau_box[-1]
        top = sv1[0:1] + sv2[0:1]
        z = jnp.sum(jnp.where(cand >= tau, jnp.exp(cand - top), 0.0), axis=0, keepdims=True)
        s1 = s1_s[h]
        cnt = jnp.zeros(s1.shape, F32)
        for b in range(k // 2):
            cnt = cnt + jnp.where(s1 + sv2[b:b + 1] >= tau, 1.0, 0.0)
        cnt_best = jnp.zeros((1, LANES), F32)
        for b in range(k // 2, k):
            cnt_best = cnt_best + jnp.where(sv1[0:1] + sv2[b:b + 1] >= tau, 1.0, 0.0)
        cnt_ref[0, h] = _replicate_word(jnp.where(s1 == sv1[0:1], cnt + cnt_best, cnt))
        e1_ref[0, h] = _replicate_word(jnp.exp(s1 - sv1[0:1]) * (0.5 / z))


def peer_select(q_slabs, keys):
    n_slab, t, slab_w = q_slabs.shape
    nhc, _, dh = keys.shape
    nchunk = t // LANES
    def out(rows, dtype):
        spec = pl.BlockSpec((1, PEER_HEADS, rows, LANES), lambda i: (i, 0, 0, 0))
        return spec, jax.ShapeDtypeStruct((nchunk, PEER_HEADS, rows, LANES), dtype)

    outs = [out(N_KEYS, jnp.uint32), out(N_KEYS, jnp.uint32),
            out(N_KEYS // PACK, jnp.uint32), out(N_KEYS // PACK, jnp.uint32)]
    return pl.pallas_call(
        _select_kernel,
        grid=(nchunk,),
        in_specs=[pl.BlockSpec((n_slab, LANES, slab_w), lambda i: (0, i, 0)),
                  pl.BlockSpec((nhc, N_KEYS, dh), lambda i: (0, 0, 0))],
        out_specs=[spec for spec, _ in outs],
        out_shape=[sds for _, sds in outs],
        scratch_shapes=[pltpu.VMEM((PEER_HEADS, N_KEYS, LANES), F32),
                        pltpu.VMEM((nhc, PEER_TOPK, LANES), F32)],
        compiler_params=_params("parallel"),
        name="peer_select",
    )(q_slabs, keys)


GATE_ROWS = 16
MXU_PIECES = 2
REGIONS_PER_HALF = 1


def _gelu_x2(a):
    return a * (1.0 + lax.erf(a * (2.0 ** -0.5)))


def _peer_dense_kernel(xt_ref, u_ref, vt_ref, cnt_ref, e1_ref, r2_ref, e2_ref, res_ref,
                       g_ref, b_ref, o_ref, a0_s, a1_s, w0_s, w1_s, acc_s, *, alpha, te, tt, n_tiles):
    gstep = pl.program_id(1)
    n_steps = pl.num_programs(1)
    n_i = te // N_KEYS

    @pl.when(gstep == 0)
    def _init():
        for ref in (a0_s, a1_s, w0_s, w1_s, acc_s):
            ref[...] = jnp.zeros(ref.shape, ref.dtype)

    assert n_i == 4 and tt % (2 * LANES) == 0
    th = tt // 2

    def gate_block(iis, tc, rbs, i0, a_ref, w_ref):
        lanes = slice(tc * LANES, (tc + 1) * LANES)

        def row(ref, ii, h):
            r = ref[tc, h, pl.ds(i0 + ii, 1), :]
            return _unpack_rows(jnp.broadcast_to(r, (GATE_ROWS // PACK, LANES)))

        cnt = {(ii, h): row(cnt_ref, ii, h) for ii in iis for h in range(PEER_HEADS)}
        e1 = {(ii, h): row(e1_ref, ii, h) for ii in iis for h in range(PEER_HEADS)}
        zero = jnp.zeros((GATE_ROWS, LANES), BF16)
        for rb in rbs:
            keys = slice(rb * GATE_ROWS // PACK, (rb + 1) * GATE_ROWS // PACK)
            gate = {ii: zero for ii in iis}
            for h in range(PEER_HEADS):
                r2 = _unpack_rows(r2_ref[tc, h, keys, :])
                e2 = _unpack_rows(e2_ref[tc, h, keys, :])
                for ii in iis:
                    gate[ii] = gate[ii] + jnp.where(r2 < cnt[ii, h], e2, zero) * e1[ii, h]
            for ii in iis:
                out_rows = slice(ii * N_KEYS + rb * GATE_ROWS, ii * N_KEYS + (rb + 1) * GATE_ROWS)
                w_ref[out_rows, lanes] = gate[ii] * _gelu_x2(a_ref[out_rows, lanes].astype(BF16))

    def half_step(tile, a_src, w_dst, w_src, vt0, a_dst, u0, region_base):
        i0 = jnp.clip(tile, 0, n_tiles - 1) * n_i
        n_tc = tt // LANES
        def quarter(r):
            iis = (2 * (r // 2), 2 * (r // 2) + 1)
            n_rb = N_KEYS // GATE_ROWS
            units = [(tc, range(part * n_rb // 2, (part + 1) * n_rb // 2))
                     for tc in range((r % 2) * n_tc // 2, (r % 2 + 1) * n_tc // 2) for part in range(2)]
            for piece in range(MXU_PIECES):
                if r < 2:
                    tok = slice(r * th, (r + 1) * th)
                    rows = slice(piece * (d // MXU_PIECES), (piece + 1) * (d // MXU_PIECES))
                    vt = _unpack_rows(vt_ref[rows.start // PACK:rows.stop // PACK, vt0:vt0 + te])
                    acc_s[rows, tok] += jnp.dot(vt, w_src[:, tok], preferred_element_type=F32)
                else:
                    tok = slice((r - 2) * th, (r - 1) * th)
                    rows = slice(piece * (te // MXU_PIECES), (piece + 1) * (te // MXU_PIECES))
                    u = _unpack_rows(u_ref[(u0 + rows.start) // PACK:(u0 + rows.stop) // PACK, :])
                    a_dst[rows, tok] = jnp.dot(u, xt_ref[:, tok], preferred_element_type=F32)
                for tc, rbs in units[piece * len(units) // MXU_PIECES:(piece + 1) * len(units) // MXU_PIECES]:
                    gate_block(iis, tc, rbs, i0, a_src, w_dst)

        per_region = n_i // REGIONS_PER_HALF
        for region in range(REGIONS_PER_HALF):
            @pl.when(gstep < n_steps + region_base + region)
            def _region(region=region):
                for r in range(region * per_region, (region + 1) * per_region):
                    quarter(r)

    d = acc_s.shape[0]
    half_step(2 * gstep - 1, a1_s, w1_s, w0_s, 0, a0_s, 0, 0)
    half_step(2 * gstep, a0_s, w0_s, w1_s, te, a1_s, te, REGIONS_PER_HALF)

    @pl.when(gstep == n_steps - 1)
    def _finish():
        y = acc_s[...].T
        o_ref[...] = _layer_norm(alpha * res_ref[...] + y, g_ref[...], b_ref[...])


def _pack_table_kernel(x_ref, o_ref, *, transpose):
    x = x_ref[...]
    o_ref[...] = _pack_rows(x.T if transpose else x)


def pack_table(tables, layer, *, transpose, tile=1024):
    _, rows, cols = tables.shape
    if transpose:
        out_shape = (cols // PACK, rows)
        out_spec = pl.BlockSpec((cols // PACK, tile), lambda i: (0, i))
    else:
        out_shape = (rows // PACK, cols)
        out_spec = pl.BlockSpec((tile // PACK, cols), lambda i: (i, 0))
    return pl.pallas_call(
        functools.partial(_pack_table_kernel, transpose=transpose),
        grid=(rows // tile,),
        in_specs=[pl.BlockSpec((None, tile, cols), lambda i: (layer, i, 0))],
        out_specs=out_spec,
        out_shape=jax.ShapeDtypeStruct(out_shape, jnp.uint32),
        compiler_params=_params("parallel"),
        name="pack_table",
    )(tables)


def peer_dense(xt, u, vt, sel, res, g, b, *, alpha, tt, te):
    d, t = xt.shape
    n_exp = u.shape[0] * PACK
    n_tiles = n_exp // te
    assert n_tiles % 2 == 0
    n_pairs = n_tiles // 2
    nchunk = tt // LANES
    sel_specs = [pl.BlockSpec((nchunk,) + a.shape[1:], lambda ti, s: (ti, 0, 0, 0)) for a in sel]
    full = lambda shape: pl.BlockSpec(shape, lambda ti, s: (0, 0))
    u_spec = pl.BlockSpec((2 * te // PACK, d), lambda ti, s: (jnp.minimum(s, n_pairs - 1), 0))
    vt_spec = pl.BlockSpec((d // PACK, 2 * te), lambda ti, s: (0, jnp.maximum(s - 1, 0)))
    return pl.pallas_call(
        functools.partial(_peer_dense_kernel, alpha=alpha, te=te, tt=tt, n_tiles=n_tiles),
        grid=(t // tt, n_pairs + 1),
        in_specs=[pl.BlockSpec((d, tt), lambda ti, s: (0, ti)), u_spec, vt_spec,
                  *sel_specs,
                  pl.BlockSpec((tt, d), lambda ti, s: (ti, 0)),
                  full((1, d)), full((1, d))],
        out_specs=pl.BlockSpec((tt, d), lambda ti, s: (ti, 0)),
        out_shape=jax.ShapeDtypeStruct((t, d), F32),
        scratch_shapes=[pltpu.VMEM((te, tt), F32), pltpu.VMEM((te, tt), F32),
                        pltpu.VMEM((te, tt), BF16), pltpu.VMEM((te, tt), BF16),
                        pltpu.VMEM((d, tt), F32)],
        compiler_params=_params("parallel", "arbitrary"),
        name="peer_dense",
    )(xt, u, vt, *sel, res, g[None, :], b[None, :])


def _ple_kernel(h_ref, pp_ref, pt_ref, wg_ref, wp_ref, *o_refs, n_prompt_tiles):
    i = pl.program_id(0)
    tail = i >= n_prompt_tiles
    h = h_ref[...]
    p = jnp.where(tail, pt_ref[...], pp_ref[...])
    gate = jax.nn.sigmoid(jnp.dot(h.astype(BF16), wg_ref[...], preferred_element_type=F32))
    proj = jnp.dot(p.astype(BF16), wp_ref[...], preferred_element_type=F32)
    out = h + gate * proj
    if len(o_refs) == 1:
        o_refs[0][...] = out
    else:
        op_ref, ot_ref = o_refs

        @pl.when(jnp.logical_not(tail))
        def _():
            op_ref[...] = out

        @pl.when(tail)
        def _():
            ot_ref[...] = out


def ple_add(h, p_prompt, layer, p_tail, wg, wp, *, tm, split):
    m, d = h.shape
    pd = p_tail.shape[1]
    n_p = m // tm - 1
    prompt_tile = lambda i: jnp.minimum(i, n_p - 1)
    full = lambda shape: pl.BlockSpec(shape, lambda i: (0, 0))
    if split:
        out_specs = [pl.BlockSpec((tm, d), lambda i: (prompt_tile(i), 0)), full((tm, d))]
        out_shape = [jax.ShapeDtypeStruct((n_p * tm, d), F32), jax.ShapeDtypeStruct((tm, d), F32)]
    else:
        out_specs = pl.BlockSpec((tm, d), lambda i: (i, 0))
        out_shape = jax.ShapeDtypeStruct((m, d), F32)
    return pl.pallas_call(
        functools.partial(_ple_kernel, n_prompt_tiles=n_p),
        grid=(m // tm,),
        in_specs=[pl.BlockSpec((tm, d), lambda i: (i, 0)),
                  pl.BlockSpec((None, tm, pd), lambda i: (layer, prompt_tile(i), 0)),
                  full((tm, pd)), full((d, d)), full((pd, d))],
        out_specs=out_specs,
        out_shape=out_shape,
        compiler_params=_params("arbitrary"),
        name="ple_add",
    )(h, p_prompt, p_tail, wg, wp)


TOKEN_TILE = 512
PEER_EXPERT_TILE = 512


def _largest_tile(n, candidates):
    for c in candidates:
        if n % c == 0:
            return c
    raise ValueError(f"no tile in {candidates} divides {n}")


def kernel(x_prompt, x_sample, cache_k, cache_v, state_conv, page_table, p_prompt, p_sample,
           ln_g, ln_b, w_attn_qkv, w_attn_lambda, attn_subln_g, w_attn_o,
           w_conv_in, conv_w, w_conv_out, w_peer_q, peer_keys, peer_u, peer_v,
           w_ple_gate, w_ple_proj):
    batch, seq, d = x_prompt.shape
    bs = x_sample.shape[0]
    assert x_sample.shape[1] == 1
    depth = ln_g.shape[0]
    tp = batch * seq
    t = tp + bs
    tm = TOKEN_TILE
    t_pad = -(-t // tm) * tm
    alpha = (2 * depth) ** 0.25
    width = ATT_HEADS * ATT_W
    tseq = _largest_tile(seq, (512, 256, 128))
    tflash = _largest_tile(seq, (1024, 512, 256, 128))

    assert tp % tm == 0 and t_pad == tp + tm

    def tail_tile(sample_rows):
        pad = jnp.zeros((tm - bs, sample_rows.shape[1]), F32)
        return jnp.concatenate([sample_rows, pad], axis=0)

    h = None
    slopes = jnp.exp2(-8.0 * jnp.arange(1, ATT_HEADS + 1, dtype=F32) / ATT_HEADS)
    slopes_b = jnp.broadcast_to(slopes[:, None, None], (ATT_HEADS, 1, tflash))

    kp_l, vp_l, ks_l, vs_l, cp_l, cs_l = [], [], [], [], [], []
    for i in range(depth):
        j = i // N_MIXERS
        if i % N_MIXERS == 0:
            lam_init = 0.8 - 0.6 * math.exp(-0.3 * i)
            w_qkv = w_attn_qkv[j].astype(BF16)
            rows_p = x_prompt.reshape(tp, d) if h is None else h[:tp]
            rows_s = x_sample.reshape(bs, d) if h is None else h[tp:t]
            qkv, k_heads, v_heads = qkv_project(rows_p, w_qkv, tm=tseq)
            qkv_s = matmul_slabs(rows_s, w_qkv, tm=bs, tn=width).reshape(3, bs, ATT_HEADS, ATT_W)
            g_sub = attn_subln_g[j]
            o_p = flash_prompt(qkv, slopes_b, w_attn_lambda[j], g_sub[None, :], batch=batch, seq=seq,
                               tq=tflash, tk=tflash, lam_init=lam_init)
            o_s = decode_sample(page_table, qkv_s[0], qkv_s[1], qkv_s[2], cache_k, cache_v,
                                w_attn_lambda[j], g_sub, layer=j, lam_init=lam_init)
            h = mm_res_ln(o_p, tail_tile(o_s.reshape(bs, width)), w_attn_o[j].astype(BF16),
                          rows_p, tail_tile(rows_s), ln_g[i, 0], ln_b[i, 0], alpha=alpha, tm=tm)
            kp_l.append(k_heads.reshape(batch, seq, ATT_HEADS, ATT_W))
            vp_l.append(v_heads.reshape(batch, seq, ATT_HEADS, ATT_W))
            ks_l.append(qkv_s[1].reshape(bs, 1, ATT_HEADS, ATT_W))
            vs_l.append(qkv_s[2].reshape(bs, 1, ATT_HEADS, ATT_W))
        else:
            if h is None:
                h = jnp.concatenate([x_prompt.reshape(tp, d), tail_tile(x_sample.reshape(bs, d))], axis=0)
            w_in = w_conv_in[j].astype(BF16)
            w_out = w_conv_out[j].astype(BF16)
            h_new, tail = conv_prompt(h, w_in, conv_w[j], w_out, ln_g[i, 0], ln_b[i, 0],
                                      batch=batch, seq=seq, alpha=alpha, tm=tseq)
            left = state_conv[j]
            h, u_tile = conv_sample(h, h_new, tail_tile(left[:, 0]), tail_tile(left[:, 1]), w_in,
                                    conv_w[j], w_out, ln_g[i, 0], ln_b[i, 0], alpha=alpha, tm=tm)
            cp_l.append(tail[:, CARRY_ROWS - (CONV_W - 1):])
            cs_l.append(jnp.stack([left[:, 1], u_tile[:bs]], axis=1))

        n_hc = 2 * PEER_HEADS
        dh = peer_keys.shape[-1]
        q_slabs = matmul_slabs(h, w_peer_q[i].astype(BF16), tm=tm, tn=d)
        keys = peer_keys[i].reshape(n_hc, N_KEYS, dh).astype(BF16)
        sel = peer_select(q_slabs, keys)
        u_packed = pack_table(peer_u, i, transpose=False)
        vt_packed = pack_table(peer_v, i, transpose=True)
        h = peer_dense(h.T.astype(BF16), u_packed, vt_packed, sel, h,
                       ln_g[i, 1], ln_b[i, 1], alpha=alpha, tt=tm, te=PEER_EXPERT_TILE)

        last = i == depth - 1
        out = ple_add(h, p_prompt.reshape(depth, tp, -1), i, tail_tile(p_sample[i].reshape(bs, -1)),
                      w_ple_gate[i].astype(BF16), w_ple_proj[i].astype(BF16), tm=tm, split=last)
        if not last:
            h = out
    h_prompt, h_tail = out

    return (h_prompt.reshape(batch, seq, d), h_tail[:bs].reshape(bs, 1, d),
            jnp.stack(kp_l), jnp.stack(vp_l), jnp.stack(ks_l), jnp.stack(vs_l),
            jnp.stack(cp_l), jnp.stack(cs_l))
```
